```python
import math
import jax
import jax.numpy as jnp
from jax import lax
import numpy as np

D_MODEL = 2048
BATCH = 32
SEQ = 256
DEPTH = 4
DEC_BATCH = 2
DEC_SEQ = 1024
PAST_LEN = 256

GRID_W = 64
N_MIXERS = 4
N_HY = len(range(0, DEPTH, N_MIXERS))
N_HG = len(range(1, DEPTH, N_MIXERS))
N_SSD = len(range(2, DEPTH, N_MIXERS))
N_AT = len(range(3, DEPTH, N_MIXERS))
N_MOD = 6
EPS = 1e-6

HY_ORDER = 2
HY_SHORT_W = 3
HY_EMB_DIM = 33
HY_BANDS = (HY_EMB_DIM - 1) // 2
HY_FILTER_W = 64
HY_N_SIN = 3
HY_DECAY_TARGET = 1e-2
HY_FAST_DECAY = 0.3
HY_SLOW_DECAY = 1.5

HG_EXPAND = 128
HG_HEADS = D_MODEL // HG_EXPAND
HG_DK = HG_EXPAND
HG_DV = D_MODEL // HG_HEADS
HG_FDIM = HG_HEADS * HG_DK
HG_IN_DIM = 3 * HG_FDIM + 2 * D_MODEL
HG_CHUNK = 16

SSD_INNER = 2 * D_MODEL
SSD_HEADDIM = 64
SSD_HEADS = SSD_INNER // SSD_HEADDIM
SSD_GROUPS = 8
SSD_STATE = 128
SSD_CONV_W = 5
SSD_GN = SSD_GROUPS * SSD_STATE
SSD_CONV_DIM = SSD_INNER + 2 * SSD_GN
SSD_IN_DIM = SSD_INNER + SSD_CONV_DIM + 2 * SSD_HEADS
SSD_CHUNK = 64

AT_HEAD_DIM = 64
AT_HEADS = D_MODEL // AT_HEAD_DIM
AT_KV_HEADS = AT_HEADS // 8
AT_REP = AT_HEADS // AT_KV_HEADS
AT_WINDOW = 128
AT_BLOCK = 128
ROPE_THETA = 10000.0

N_EXPERTS = 32
TOP_K = 4
D_EXPERT = D_MODEL
SWIGLU_ALPHA = 1.702
SWIGLU_LIMIT = 7.0
MOE_BLOCK = 256

kernel_name = 'hybrid_prefix_diffusion_trunk_step'


def rmsnorm(x, g):
    xf = x.astype(jnp.float32)
    y = xf * lax.rsqrt(jnp.mean(xf * xf, axis=-1, keepdims=True) + EPS)
    return (y * g.astype(jnp.float32)).astype(x.dtype)


def ada_mod(cvec, w, b):
    m = (jax.nn.silu(cvec) @ w + b).reshape(cvec.shape[0], N_MOD, 1, D_MODEL)
    return [m[:, i] for i in range(N_MOD)]


def centred_dwconv(x, w):
    k = w.shape[0]
    return lax.conv_general_dilated(x, w.astype(x.dtype)[:, None, :], window_strides=(1,),
                                    padding=[(k // 2, k // 2)], dimension_numbers=('NWC', 'WIO', 'NWC'),
                                    feature_group_count=x.shape[-1])


def hyena_filter_spectrum(L, w1, b1, w_hid, b_hid, freq, w_filt_out):
    t = jnp.linspace(0.0, 1.0, L, dtype=jnp.float32)[:, None]
    om = 2.0 * math.pi * jnp.arange(L, dtype=jnp.float32)[:, None] / L
    f = jnp.linspace(1e-4, HY_BANDS - 1, HY_BANDS, dtype=jnp.float32)[None, :]
    z = jnp.concatenate([t, jnp.cos(f * om), -jnp.sin(f * om)], axis=-1)
    h = jnp.sin(freq[0] * (z @ w1 + b1))
    for s in range(HY_N_SIN - 1):
        h = jnp.sin(freq[s + 1] * (h @ w_hid[s] + b_hid[s]))
    h = (h @ w_filt_out).astype(jnp.float32).reshape(L, HY_ORDER, 2, D_MODEL)
    max_decay = math.log(HY_DECAY_TARGET) / HY_FAST_DECAY
    min_decay = math.log(HY_DECAY_TARGET) / HY_SLOW_DECAY
    deltas = jnp.abs(jnp.linspace(min_decay, max_decay, D_MODEL, dtype=jnp.float32))
    h = h * jnp.exp(-t.reshape(L, 1, 1, 1) * deltas)
    h_fwd, h_bwd = h[:, :, 0], h[:, :, 1]
    two_sided = jnp.concatenate([h_fwd, jnp.zeros_like(h_fwd[:1]), h_bwd[:0:-1]], axis=0)
    return jnp.fft.rfft(two_sided, axis=0)


def hyena_mixer(h, w_in, w_short, w1, b1, w_hid, b_hid, freq, w_filt_out, skip, w_out):
    L = h.shape[1]
    u = centred_dwconv(h @ w_in, w_short).astype(jnp.float32)
    v, x1, x2 = jnp.split(u, 3, axis=-1)
    k_f = hyena_filter_spectrum(L, w1, b1, w_hid, b_hid, freq, w_filt_out)
    z = v
    for o, gate in enumerate((x1, x2)):
        z_f = jnp.fft.rfft(z, n=2 * L, axis=1)
        conv = jnp.fft.irfft(z_f * k_f[None, :, o], n=2 * L, axis=1)[:, :L]
        z = gate * (conv + z * skip[o])
    return z.astype(h.dtype) @ w_out


def gla_chunk_scan(q, k, v, log_g, s0):
    B, L = q.shape[:2]
    n = L // HG_CHUNK

    def chunks(a):
        return jnp.moveaxis(a.reshape(B, n, HG_CHUNK, *a.shape[2:]), 1, 0)

    tril = jnp.tril(jnp.ones((HG_CHUNK, HG_CHUNK), dtype=bool))[None, :, :, None, None]

    def step(S, inp):
        qc, kc, vc, gc = inp
        G = jnp.cumsum(gc, axis=1)
        diff = G[:, :, None] - G[:, None, :]
        decay = jnp.where(tril, jnp.exp(jnp.where(tril, diff, 0.0)), 0.0)
        att = jnp.einsum('bthk,btshk,bshk->bhts', qc, decay, kc)
        o = jnp.einsum('bhts,bshv->bthv', att, vc) + jnp.einsum('bthk,bhkv->bthv', qc * jnp.exp(G), S)
        g_last = G[:, -1]
        S = jnp.exp(g_last)[..., None] * S + jnp.einsum('bshk,bshv->bhkv', kc * jnp.exp(g_last[:, None] - G), vc)
        return S, o

    S, o = lax.scan(step, s0, (chunks(q), chunks(k), chunks(v), chunks(log_g)))
    return jnp.moveaxis(o, 0, 1).reshape(B, L, HG_HEADS, -1), S


def hgrn2_mixer(h, s0_f, s0_b, w_in, lb, g_norm, w_o):
    B, L, _ = h.shape
    proj = (h @ w_in).astype(jnp.float32)
    q = jax.nn.silu(proj[..., :HG_FDIM]).reshape(B, L, HG_HEADS, HG_DK)
    f = proj[..., HG_FDIM:3 * HG_FDIM].reshape(B, L, 2, HG_HEADS, HG_DK)
    i = proj[..., 3 * HG_FDIM:3 * HG_FDIM + D_MODEL].reshape(B, L, HG_HEADS, HG_DV)
    gate = proj[..., 3 * HG_FDIM + D_MODEL:].reshape(B, L, HG_HEADS, HG_DV)
    lb = lb.astype(jnp.float32).reshape(2, HG_HEADS, HG_DK)
    o = jnp.zeros((B, L, HG_HEADS, HG_DV), jnp.float32)
    finals = []
    for d, s0 in enumerate((s0_f, s0_b)):
        fd = f[:, :, d]
        log_g = jnp.logaddexp(jnp.log(lb[d]), jnp.log1p(-lb[d]) + jax.nn.log_sigmoid(fd))
        k = (1.0 - lb[d]) * jax.nn.sigmoid(-fd)
        seq = (q, k, i, log_g)
        if d == 1:
            seq = tuple(jnp.flip(a, axis=1) for a in seq)
        od, sd = gla_chunk_scan(seq[0], seq[1], seq[2], seq[3], s0.astype(jnp.float32))
        o = o + (od if d == 0 else jnp.flip(od, axis=1))
        finals.append(sd)
    o = rmsnorm(o, g_norm) * jax.nn.silu(gate)
    return o.reshape(B, L, D_MODEL).astype(h.dtype) @ w_o, finals[0], finals[1]


def ssd_chunk_scan(x, dt, a, bm, cm, s0):
    B, L, H, P = x.shape
    n, Q, G, R, N = L // SSD_CHUNK, SSD_CHUNK, SSD_GROUPS, H // SSD_GROUPS, SSD_STATE
    x = x.reshape(B, n, Q, G, R, P)
    dt = dt.reshape(B, n, Q, G, R)
    bm = bm.reshape(B, n, Q, G, N)
    cm = cm.reshape(B, n, Q, G, N)
    a_cs = jnp.cumsum(dt * a.reshape(G, R), axis=2)
    tril = jnp.tril(jnp.ones((Q, Q), dtype=bool))[None, None, :, :, None, None]
    seg = a_cs[:, :, :, None] - a_cs[:, :, None, :]
    lmat = jnp.where(tril, jnp.exp(jnp.where(tril, seg, 0.0)), 0.0)
    cb = jnp.einsum('bclgn,bcsgn->bclsg', cm, bm)
    w = cb[..., None] * lmat * dt[:, :, None]
    y_diag = jnp.einsum('bclsgr,bcsgrp->bclgrp', w, x)
    decay_states = jnp.exp(a_cs[:, :, -1:] - a_cs)
    states = jnp.einsum('bcsgn,bcsgr,bcsgrp->bcgrpn', bm, decay_states * dt, x)
    states = jnp.concatenate([s0.reshape(B, 1, G, R, P, N), states], axis=1)
    cs = jnp.cumsum(jnp.pad(a_cs[:, :, -1], ((0, 0), (1, 0), (0, 0), (0, 0))), axis=1)
    ctril = jnp.tril(jnp.ones((n + 1, n + 1), dtype=bool))[None, :, :, None, None]
    cseg = cs[:, :, None] - cs[:, None, :]
    decay_chunk = jnp.where(ctril, jnp.exp(jnp.where(ctril, cseg, 0.0)), 0.0)
    new_states = jnp.einsum('bzcgr,bcgrpn->bzgrpn', decay_chunk, states)
    y_off = jnp.einsum('bclgn,bcgrpn,bclgr->bclgrp', cm, new_states[:, :-1], jnp.exp(a_cs))
    y = (y_diag + y_off).reshape(B, L, H, P)
    return y, new_states[:, -1].reshape(B, H, P, N)


def ssd_mixer(h, s0_f, s0_b, w_in, conv_w, conv_b, dt_bias, a_log, d_skip, norm_w, w_out):
    B, L, _ = h.shape
    proj = h @ w_in
    z = proj[..., :SSD_INNER].astype(jnp.float32)
    xbc = proj[..., SSD_INNER:SSD_INNER + SSD_CONV_DIM]
    dt_raw = proj[..., SSD_INNER + SSD_CONV_DIM:].astype(jnp.float32).reshape(B, L, 2, SSD_HEADS)
    xbc = jax.nn.silu((centred_dwconv(xbc, conv_w) + conv_b).astype(jnp.float32))
    x = xbc[..., :SSD_INNER].reshape(B, L, SSD_HEADS, SSD_HEADDIM)
    bm = xbc[..., SSD_INNER:SSD_INNER + SSD_GN].reshape(B, L, SSD_GROUPS, SSD_STATE)
    cm = xbc[..., SSD_INNER + SSD_GN:].reshape(B, L, SSD_GROUPS, SSD_STATE)
    y = x * d_skip.astype(jnp.float32)[:, None]
    finals = []
    for d, s0 in enumerate((s0_f, s0_b)):
        dt = jax.nn.softplus(dt_raw[:, :, d] + dt_bias[d])
        a = -jnp.exp(a_log[d].astype(jnp.float32))
        seq = (x, dt, bm, cm)
        if d == 1:
            seq = tuple(jnp.flip(t, axis=1) for t in seq)
        yd, sd = ssd_chunk_scan(seq[0], seq[1], a, seq[2], seq[3], s0.astype(jnp.float32))
        y = y + (yd if d == 0 else jnp.flip(yd, axis=1))
        finals.append(sd)
    gs = SSD_INNER // SSD_GROUPS
    y = y.reshape(B, L, SSD_GROUPS, gs) * jax.nn.silu(z).reshape(B, L, SSD_GROUPS, gs)
    y = rmsnorm(y, norm_w.reshape(SSD_GROUPS, gs)).reshape(B, L, SSD_INNER)
    return y.astype(h.dtype) @ w_out, finals[0], finals[1]


def axial_rope(x):
    L = x.shape[1]
    rows = L // GRID_W
    row = jnp.repeat(jnp.arange(rows), GRID_W).astype(jnp.float32)
    col = jnp.tile(jnp.arange(GRID_W), rows).astype(jnp.float32)
    half = AT_HEAD_DIM // 2
    inv = ROPE_THETA ** (-jnp.arange(0, half, 2, dtype=jnp.float32) / half)

    def rot(xa, pos):
        ang = pos[:, None] * inv
        cos, sin = jnp.cos(ang)[None, :, None], jnp.sin(ang)[None, :, None]
        x1, x2 = jnp.split(xa, 2, axis=-1)
        return jnp.concatenate([x1 * cos - x2 * sin, x2 * cos + x1 * sin], axis=-1)

    xf = x.astype(jnp.float32)
    return jnp.concatenate([rot(xf[..., :half], row), rot(xf[..., half:], col)], axis=-1)


def qkv_proj(h, w_qkv):
    B, L, _ = h.shape
    nq, nk = AT_HEADS * AT_HEAD_DIM, AT_KV_HEADS * AT_HEAD_DIM
    qkv = (h @ w_qkv).astype(jnp.float32)
    q = qkv[..., :nq].reshape(B, L, AT_HEADS, AT_HEAD_DIM)
    k = qkv[..., nq:nq + nk].reshape(B, L, AT_KV_HEADS, AT_HEAD_DIM)
    v = qkv[..., nq + nk:].reshape(B, L, AT_KV_HEADS, AT_HEAD_DIM)
    return q, k, v


def attn_context(h, w_qkv, sink, w_o):
    B, L, _ = h.shape
    q, k, v = qkv_proj(h, w_qkv)
    nb = L // AT_BLOCK
    qb = jnp.moveaxis(q.reshape(B, nb, AT_BLOCK, AT_KV_HEADS, AT_REP, AT_HEAD_DIM), 1, 0)
    sk = sink.astype(jnp.float32).reshape(1, AT_KV_HEADS, AT_REP, 1, 1)
    scale = AT_HEAD_DIM ** -0.5

    def block(qi):
        s = jnp.einsum('bqhrd,bkhd->bhrqk', qi, k) * scale
        s = jnp.concatenate([s, jnp.broadcast_to(sk, s.shape[:-1] + (1,))], axis=-1)
        p = jax.nn.softmax(s, axis=-1)[..., :-1]
        return jnp.einsum('bhrqk,bkhd->bqhrd', p, v)

    o = jnp.moveaxis(lax.map(block, qb), 0, 1).reshape(B, L, AT_HEADS * AT_HEAD_DIM)
    return o.astype(h.dtype) @ w_o, k.astype(h.dtype), v.astype(h.dtype)


def attn_latent(h, k_ctx, v_ctx, w_qkv, sink, w_o):
    B, L, _ = h.shape
    q, k, v = qkv_proj(h, w_qkv)
    q, k = axial_rope(q), axial_rope(k)
    nb = L // AT_BLOCK
    qb = q.reshape(B, nb, AT_BLOCK, AT_KV_HEADS, AT_REP, AT_HEAD_DIM)

    def band(a):
        ap = jnp.pad(a, ((0, 0), (AT_BLOCK, AT_BLOCK), (0, 0), (0, 0)))
        return jnp.concatenate([ap[:, s * AT_BLOCK:s * AT_BLOCK + L].reshape(B, nb, AT_BLOCK, AT_KV_HEADS, AT_HEAD_DIM)
                                for s in range(3)], axis=2)

    kb, vb = band(k), band(v)
    blk = jnp.arange(nb)[:, None] * AT_BLOCK
    qpos = blk + jnp.arange(AT_BLOCK)
    kpos = blk - AT_BLOCK + jnp.arange(3 * AT_BLOCK)
    mask = ((jnp.abs(qpos[:, :, None] - kpos[:, None, :]) <= AT_WINDOW)
            & (kpos[:, None, :] >= 0) & (kpos[:, None, :] < L))
    scale = AT_HEAD_DIM ** -0.5
    kc, vc = k_ctx.astype(jnp.float32), v_ctx.astype(jnp.float32)
    s_lat = jnp.einsum('bnqhrd,bnkhd->bnhrqk', qb, kb) * scale
    s_lat = jnp.where(mask[None, :, None, None], s_lat, -jnp.inf)
    s_ctx = jnp.einsum('bnqhrd,bchd->bnhrqc', qb, kc) * scale
    sk = jnp.broadcast_to(sink.astype(jnp.float32).reshape(1, 1, AT_KV_HEADS, AT_REP, 1, 1), s_lat.shape[:-1] + (1,))
    p = jax.nn.softmax(jnp.concatenate([s_lat, s_ctx, sk], axis=-1), axis=-1)
    nk, lc = 3 * AT_BLOCK, kc.shape[1]
    o = (jnp.einsum('bnhrqk,bnkhd->bnqhrd', p[..., :nk], vb)
         + jnp.einsum('bnhrqc,bchd->bnqhrd', p[..., nk:nk + lc], vc))
    return o.reshape(B, L, AT_HEADS * AT_HEAD_DIM).astype(h.dtype) @ w_o


def moe_ffn(x, w_router, b_router, w_in, b_in, w_out, b_out):
    shape = x.shape
    xt = x.reshape(-1, D_MODEL)
    n_tok = xt.shape[0]
    logits = (xt @ w_router + b_router).astype(jnp.float32)
    top_v, top_i = lax.top_k(logits, TOP_K)
    gates = jax.nn.softmax(top_v, axis=-1)
    n_slot = n_tok * TOP_K
    n_blocks = -(-(n_slot + N_EXPERTS * (MOE_BLOCK - 1)) // MOE_BLOCK)
    n_rows = n_blocks * MOE_BLOCK
    flat_e = top_i.reshape(-1)
    flat_t = jnp.repeat(jnp.arange(n_tok, dtype=jnp.int32), TOP_K)
    order = jnp.argsort(flat_e)
    sorted_e = flat_e[order]
    counts = jnp.bincount(flat_e, length=N_EXPERTS)
    padded = (counts + MOE_BLOCK - 1) // MOE_BLOCK * MOE_BLOCK
    start = jnp.cumsum(counts) - counts
    pend = jnp.cumsum(padded)
    pstart = pend - padded
    dest = pstart[sorted_e] + jnp.arange(n_slot) - start[sorted_e]
    tok_buf = jnp.full((n_rows,), n_tok, jnp.int32).at[dest].set(flat_t[order])
    gate_buf = jnp.zeros((n_rows,), jnp.float32).at[dest].set(gates.reshape(-1)[order])
    blk_e = jnp.minimum(jnp.searchsorted(pend, jnp.arange(n_blocks) * MOE_BLOCK, side='right'), N_EXPERTS - 1)
    x_pad = jnp.concatenate([xt, jnp.zeros((1, D_MODEL), xt.dtype)], axis=0)
    xb = x_pad[tok_buf].reshape(n_blocks, MOE_BLOCK, D_MODEL)

    def expert_block(args):
        xe, e = args
        hgu = xe @ w_in[e] + b_in[e]
        g = jnp.minimum(hgu[:, :D_EXPERT], SWIGLU_LIMIT)
        u = jnp.clip(hgu[:, D_EXPERT:], -SWIGLU_LIMIT, SWIGLU_LIMIT)
        return ((u + 1.0) * g * jax.nn.sigmoid(SWIGLU_ALPHA * g)) @ w_out[e] + b_out[e]

    yb = lax.map(expert_block, (xb, blk_e))
    y = jnp.zeros((n_tok + 1, D_MODEL), jnp.float32).at[tok_buf].add(yb.reshape(n_rows, D_MODEL) * gate_buf[:, None])
    return y[:n_tok].reshape(shape).astype(x.dtype)


def setup_inputs(seed: int = 0) -> dict:
    key = jax.random.key(seed)
    keys = iter(jax.random.split(key, 64))
    D = D_MODEL

    def nrm(shape, scale=1.0):
        return jax.random.normal(next(keys), shape, jnp.float32) * scale

    def gain(shape):
        return 1.0 + nrm(shape, 0.05)

    dt0 = jnp.exp(jax.random.uniform(next(keys), (N_SSD, 2, SSD_HEADS), jnp.float32, math.log(1e-3), math.log(1e-1)))
    a0 = jax.random.uniform(next(keys), (N_SSD, 2, SSD_HEADS), jnp.float32, 1.0, 16.0)
    return {
        'x_prompt': nrm((BATCH, SEQ, D)),
        'x_sample': nrm((DEC_BATCH, DEC_SEQ, D)),
        'cache_attn_k': nrm((DEC_BATCH, N_AT, PAST_LEN, AT_KV_HEADS, AT_HEAD_DIM)),
        'cache_attn_v': nrm((DEC_BATCH, N_AT, PAST_LEN, AT_KV_HEADS, AT_HEAD_DIM)),
        'state_hgrn': nrm((DEC_BATCH, N_HG, 2, HG_HEADS, HG_DK, HG_DV), 0.3),
        'state_ssd': nrm((DEC_BATCH, N_SSD, 2, SSD_HEADS, SSD_HEADDIM, SSD_STATE), 0.1),
        'c': nrm((DEC_BATCH, D)),
        'c_ctx': nrm((D,)),
        'ada_w': nrm((DEPTH, D, N_MOD * D), 0.5 * D ** -0.5),
        'ada_b': nrm((DEPTH, N_MOD * D), 0.02),
        'norm_mix': gain((DEPTH, D)),
        'norm_ffn': gain((DEPTH, D)),
        'norm_final': gain((D,)),
        'hy_w_in': nrm((N_HY, D, 3 * D), D ** -0.5),
        'hy_w_short': nrm((N_HY, HY_SHORT_W, 3 * D), 0.5),
        'hy_filt_w1': nrm((N_HY, HY_EMB_DIM, HY_FILTER_W), HY_EMB_DIM ** -0.5),
        'hy_filt_b1': nrm((N_HY, HY_FILTER_W), 0.1),
        'hy_filt_w_hid': nrm((N_HY, HY_N_SIN - 1, HY_FILTER_W, HY_FILTER_W), HY_FILTER_W ** -0.5),
        'hy_filt_b_hid': nrm((N_HY, HY_N_SIN - 1, HY_FILTER_W), 0.1),
        'hy_filt_freq': gain((N_HY, HY_N_SIN, HY_FILTER_W)),
        'hy_filt_w_out': nrm((N_HY, HY_FILTER_W, HY_ORDER * 2 * D), 0.05 * HY_FILTER_W ** -0.5),
        'hy_skip': nrm((N_HY, HY_ORDER, D), 0.5),
        'hy_w_out': nrm((N_HY, D, D), D ** -0.5),
        'hg_w_in': nrm((N_HG, D, HG_IN_DIM), D ** -0.5),
        'hg_lb': nrm((2, DEPTH, HG_FDIM), 0.1),
        'hg_norm': gain((N_HG, HG_DV)),
        'hg_w_o': nrm((N_HG, D, D), D ** -0.5),
        'ssd_w_in': nrm((N_SSD, D, SSD_IN_DIM), D ** -0.5),
        'ssd_conv_w': nrm((N_SSD, SSD_CONV_W, SSD_CONV_DIM), SSD_CONV_W ** -0.5),
        'ssd_conv_b': nrm((N_SSD, SSD_CONV_DIM), 0.02),
        'ssd_dt_bias': dt0 + jnp.log(-jnp.expm1(-dt0)),
        'ssd_a_log': jnp.log(a0),
        'ssd_d': gain((N_SSD, SSD_HEADS)),
        'ssd_norm': gain((N_SSD, SSD_INNER)),
        'ssd_w_out': nrm((N_SSD, SSD_INNER, D), SSD_INNER ** -0.5),
        'at_w_qkv': nrm((N_AT, D, (AT_HEADS + 2 * AT_KV_HEADS) * AT_HEAD_DIM), D ** -0.5),
        'at_sink': nrm((N_AT, AT_HEADS), 0.5),
        'at_w_o': nrm((N_AT, AT_HEADS * AT_HEAD_DIM, D), (AT_HEADS * AT_HEAD_DIM) ** -0.5),
        'moe_w_router': nrm((DEPTH, D, N_EXPERTS), D ** -0.5),
        'moe_b_router': nrm((DEPTH, N_EXPERTS), 0.01),
        'moe_w_in': nrm((DEPTH, N_EXPERTS, D, 2 * D_EXPERT), D ** -0.5),
        'moe_b_in': nrm((DEPTH, N_EXPERTS, 2 * D_EXPERT), 0.02),
        'moe_w_out': nrm((DEPTH, N_EXPERTS, D_EXPERT, D), D_EXPERT ** -0.5),
        'moe_b_out': nrm((DEPTH, N_EXPERTS, D), 0.02),
    }


def reference(x_prompt, x_sample, cache_attn_k, cache_attn_v, state_hgrn, state_ssd, c, c_ctx,
              ada_w, ada_b, norm_mix, norm_ffn, norm_final,
              hy_w_in, hy_w_short, hy_filt_w1, hy_filt_b1, hy_filt_w_hid, hy_filt_b_hid, hy_filt_freq,
              hy_filt_w_out, hy_skip, hy_w_out,
              hg_w_in, hg_lb, hg_norm, hg_w_o,
              ssd_w_in, ssd_conv_w, ssd_conv_b, ssd_dt_bias, ssd_a_log, ssd_d, ssd_norm, ssd_w_out,
              at_w_qkv, at_sink, at_w_o,
              moe_w_router, moe_b_router, moe_w_in, moe_b_in, moe_w_out, moe_b_out):
    lbs = jax.nn.softmax(hg_lb.astype(jnp.float32), axis=1)
    lbs = jnp.cumsum(lbs, axis=1) - lbs[:, :1]
    xp, xs = x_prompt, x_sample
    new_k, new_v, new_hg, new_ssd = [], [], [], []
    for li in range(DEPTH):
        kind, j = li % N_MIXERS, li // N_MIXERS
        mp = ada_mod(c_ctx[None], ada_w[li], ada_b[li])
        ms = ada_mod(c, ada_w[li], ada_b[li])
        hp = rmsnorm(xp, norm_mix[li]) * (1.0 + mp[1]) + mp[0]
        hs = rmsnorm(xs, norm_mix[li]) * (1.0 + ms[1]) + ms[0]
        if kind == 0:
            hy = (hy_w_in[j], hy_w_short[j], hy_filt_w1[j], hy_filt_b1[j], hy_filt_w_hid[j], hy_filt_b_hid[j],
                  hy_filt_freq[j], hy_filt_w_out[j], hy_skip[j], hy_w_out[j])
            yp = hyena_mixer(hp, *hy)
            ys = hyena_mixer(hs, *hy)
        elif kind == 1:
            hg = (hg_w_in[j], lbs[:, li], hg_norm[j], hg_w_o[j])
            s0 = jnp.zeros((xp.shape[0], HG_HEADS, HG_DK, HG_DV), jnp.float32)
            yp, sf, sb = hgrn2_mixer(hp, s0, s0, *hg)
            ys, _, _ = hgrn2_mixer(hs, state_hgrn[:, j, 0], state_hgrn[:, j, 1], *hg)
            new_hg.append(jnp.stack([sf, sb], axis=1).astype(xp.dtype))
        elif kind == 2:
            sp = (ssd_w_in[j], ssd_conv_w[j], ssd_conv_b[j], ssd_dt_bias[j], ssd_a_log[j], ssd_d[j], ssd_norm[j], ssd_w_out[j])
            s0 = jnp.zeros((xp.shape[0], SSD_HEADS, SSD_HEADDIM, SSD_STATE), jnp.float32)
            yp, sf, sb = ssd_mixer(hp, s0, s0, *sp)
            ys, _, _ = ssd_mixer(hs, state_ssd[:, j, 0], state_ssd[:, j, 1], *sp)
            new_ssd.append(jnp.stack([sf, sb], axis=1).astype(xp.dtype))
        else:
            yp, kc, vc = attn_context(hp, at_w_qkv[j], at_sink[j], at_w_o[j])
            ys = attn_latent(hs, cache_attn_k[:, j], cache_attn_v[:, j], at_w_qkv[j], at_sink[j], at_w_o[j])
            new_k.append(kc)
            new_v.append(vc)
        xp = xp + mp[2] * yp
        xs = xs + ms[2] * ys
        moe = (moe_w_router[li], moe_b_router[li], moe_w_in[li], moe_b_in[li], moe_w_out[li], moe_b_out[li])
        xp = xp + mp[5] * moe_ffn(rmsnorm(xp, norm_ffn[li]) * (1.0 + mp[4]) + mp[3], *moe)
        xs = xs + ms[5] * moe_ffn(rmsnorm(xs, norm_ffn[li]) * (1.0 + ms[4]) + ms[3], *moe)
    y_prompt = rmsnorm(xp, norm_final)
    y_sample = rmsnorm(xs, norm_final)
    new_attn_k = jnp.stack(new_k, axis=1)
    new_attn_v = jnp.stack(new_v, axis=1)
    new_state_hgrn = jnp.stack(new_hg, axis=1)
    new_state_ssd = jnp.stack(new_ssd, axis=1)
    return (y_prompt, y_sample, new_attn_k, new_attn_v, new_state_hgrn, new_state_ssd)
```

```python
import functools
import math

import numpy as np
import jax
import jax.numpy as jnp
from jax import lax
from jax.experimental import pallas as pl
from jax.experimental.pallas import tpu as pltpu

F32 = jnp.float32
BF16 = jnp.bfloat16
HIGHEST = lax.Precision.HIGHEST

D_MODEL = 2048
BATCH = 32
SEQ = 256
DEPTH = 4
DEC_BATCH = 2
DEC_SEQ = 1024
PAST_LEN = 256
GRID_W = 64
N_MIXERS = 4
N_MOD = 6
EPS = 1e-6

HY_ORDER = 2
HY_SHORT_W = 3
HY_EMB_DIM = 33
HY_BANDS = (HY_EMB_DIM - 1) // 2
HY_FILTER_W = 64
HY_N_SIN = 3
HY_DECAY_TARGET = 1e-2
HY_FAST_DECAY = 0.3
HY_SLOW_DECAY = 1.5

HG_HEADS = D_MODEL // 128
HG_DK = 128
HG_DV = D_MODEL // HG_HEADS
HG_FDIM = HG_HEADS * HG_DK
HG_IN_DIM = 3 * HG_FDIM + 2 * D_MODEL
HG_CHUNK = 16

SSD_INNER = 2 * D_MODEL
SSD_HEADDIM = 64
SSD_HEADS = SSD_INNER // SSD_HEADDIM
SSD_GROUPS = 8
SSD_STATE = 128
SSD_CONV_W = 5
SSD_GN = SSD_GROUPS * SSD_STATE
SSD_CONV_DIM = SSD_INNER + 2 * SSD_GN
SSD_IN_DIM = SSD_INNER + SSD_CONV_DIM + 2 * SSD_HEADS

AT_HEAD_DIM = 64
AT_HEADS = D_MODEL // AT_HEAD_DIM
AT_KV_HEADS = AT_HEADS // 8
AT_REP = AT_HEADS // AT_KV_HEADS
AT_WINDOW = 128
AT_BLOCK = 128
ROPE_THETA = 10000.0

N_EXPERTS = 32
TOP_K = 4
D_EXPERT = D_MODEL
SWIGLU_ALPHA = 1.702
SWIGLU_LIMIT = 7.0

T_PROMPT = BATCH * SEQ
T_SAMPLE = DEC_BATCH * DEC_SEQ
T_ALL = T_PROMPT + T_SAMPLE
N_CVEC = 8

VMEM_LIMIT = 56 * 1024 * 1024


def _cparams(sem):
    return pltpu.CompilerParams(dimension_semantics=sem, vmem_limit_bytes=VMEM_LIMIT)


def _mod_row(tile_idx, tile_rows):
    start = tile_idx * tile_rows
    return jnp.where(start < T_PROMPT, 0, 1 + (start - T_PROMPT) // DEC_SEQ)


ADA_TN = 1024


def _ada_kernel(c_ref, w_ref, b_ref, o_ref):
    c = c_ref[...]
    s = c * jax.nn.sigmoid(c)
    o_ref[...] = jnp.dot(s.astype(BF16), w_ref[...].astype(BF16), preferred_element_type=F32) + b_ref[...]


def ada_mods(cvec, ada_w, ada_b):
    n = N_MOD * D_MODEL
    out = pl.pallas_call(
        _ada_kernel,
        grid=(DEPTH, n // ADA_TN),
        in_specs=[
            pl.BlockSpec((N_CVEC, D_MODEL), lambda l, j: (0, 0)),
            pl.BlockSpec((None, D_MODEL, ADA_TN), lambda l, j: (l, 0, j)),
            pl.BlockSpec((None, 1, ADA_TN), lambda l, j: (l, 0, j)),
        ],
        out_specs=pl.BlockSpec((None, N_CVEC, ADA_TN), lambda l, j: (l, 0, j)),
        out_shape=jax.ShapeDtypeStruct((DEPTH, N_CVEC, n), F32),
        compiler_params=_cparams(("arbitrary", "arbitrary")),
        name="ada_mods",
    )(cvec, ada_w, ada_b.reshape(DEPTH, 1, n))
    return out.reshape(DEPTH, N_CVEC * N_MOD, 1, D_MODEL)


def _mod_spec(li, comp, tile_rows):
    return pl.BlockSpec((None, None, 1, D_MODEL),
                        lambda i, *_: (li, _mod_row(i, tile_rows) * N_MOD + comp, 0, 0))


NORM_TM = 256


def _norm_mod(x, g, sh, sc):
    y = x * lax.rsqrt(jnp.mean(x * x, axis=-1, keepdims=True) + EPS) * g
    return y * (1.0 + sc) + sh


def _norm_kernel(x_ref, g_ref, sh_ref, sc_ref, o_ref):
    o_ref[...] = _norm_mod(x_ref[...], g_ref[...], sh_ref[...], sc_ref[...]).astype(o_ref.dtype)


def norm_modulate(x, gains, mods, li, shift_comp):
    t = x.shape[0]
    return pl.pallas_call(
        _norm_kernel,
        grid=(t // NORM_TM,),
        in_specs=[
            pl.BlockSpec((NORM_TM, D_MODEL), lambda i: (i, 0)),
            pl.BlockSpec((None, 1, D_MODEL), lambda i: (li, 0, 0)),
            _mod_spec(li, shift_comp, NORM_TM),
            _mod_spec(li, shift_comp + 1, NORM_TM),
        ],
        out_specs=pl.BlockSpec((NORM_TM, D_MODEL), lambda i: (i, 0)),
        out_shape=jax.ShapeDtypeStruct((t, D_MODEL), BF16),
        compiler_params=_cparams(("arbitrary",)),
        name="norm_modulate",
    )(x, gains.reshape(DEPTH, 1, D_MODEL), mods, mods)


def _norm_router_kernel(x_ref, g_ref, sh_ref, sc_ref, wr_ref, br_ref, o_ref, ti_ref, tg_ref):
    h = _norm_mod(x_ref[...], g_ref[...], sh_ref[...], sc_ref[...])
    o_ref[...] = _pack_bf16_pairs(h)
    logits = lax.dot_general(wr_ref[...], h, (((1,), (1,)), ((), ())),
                             precision=HIGHEST, preferred_element_type=F32) + br_ref[...]
    eidx = lax.broadcasted_iota(jnp.int32, logits.shape, 0)
    vals = logits
    top_v = []
    for k in range(TOP_K):
        m = jnp.max(vals, axis=0, keepdims=True)
        sel = jnp.min(jnp.where(vals == m, eidx, N_EXPERTS), axis=0, keepdims=True)
        top_v.append(m)
        ti_ref[k:k + 1, :] = sel
        vals = jnp.where(eidx == sel, -jnp.inf, vals)
    ex = [jnp.exp(v - top_v[0]) for v in top_v]
    den = ex[0] + ex[1] + ex[2] + ex[3]
    for k in range(TOP_K):
        tg_ref[k:k + 1, :] = ex[k] / den


def norm_router(x, gains, mods, li, w_router_t, b_router):
    t = x.shape[0]
    return pl.pallas_call(
        _norm_router_kernel,
        grid=(t // NORM_TM,),
        in_specs=[
            pl.BlockSpec((NORM_TM, D_MODEL), lambda i: (i, 0)),
            pl.BlockSpec((None, 1, D_MODEL), lambda i: (li, 0, 0)),
            _mod_spec(li, 3, NORM_TM),
            _mod_spec(li, 4, NORM_TM),
            pl.BlockSpec((None, N_EXPERTS, D_MODEL), lambda i: (li, 0, 0)),
            pl.BlockSpec((None, N_EXPERTS, 1), lambda i: (li, 0, 0)),
        ],
        out_specs=[
            pl.BlockSpec((NORM_TM, D_MODEL // 2), lambda i: (i, 0)),
            pl.BlockSpec((TOP_K, NORM_TM), lambda i: (0, i)),
            pl.BlockSpec((TOP_K, NORM_TM), lambda i: (0, i)),
        ],
        out_shape=[
            jax.ShapeDtypeStruct((t, D_MODEL // 2), jnp.uint32),
            jax.ShapeDtypeStruct((TOP_K, t), jnp.int32),
            jax.ShapeDtypeStruct((TOP_K, t), F32),
        ],
        compiler_params=_cparams(("arbitrary",)),
        name="norm_router",
    )(x, gains.reshape(DEPTH, 1, D_MODEL), mods, mods, w_router_t, b_router.reshape(DEPTH, N_EXPERTS, 1))


MM_TM = 1024


def _mm_kernel(x_ref, w_ref, o_ref, wb_ref):
    @pl.when(pl.program_id(1) == 0)
    def _():
        wb_ref[...] = w_ref[...].astype(BF16)

    o_ref[...] = jnp.dot(x_ref[...], wb_ref[...], preferred_element_type=F32).astype(o_ref.dtype)


def _mm_res_kernel(x_ref, w_ref, r_ref, g_ref, o_ref, wb_ref):
    @pl.when(pl.program_id(1) == 0)
    def _():
        wb_ref[...] = w_ref[...].astype(BF16)

    o_ref[...] = r_ref[...] + g_ref[...] * jnp.dot(x_ref[...], wb_ref[...], preferred_element_type=F32)


def _w_spec(w, wl, k, tn):
    if w.ndim == 3:
        return pl.BlockSpec((None, k, tn), lambda j, i: (wl, 0, j))
    return pl.BlockSpec((k, tn), lambda j, i: (0, j))


def matmul(x, w, wl=0, out_dtype=F32, tn=1024):
    t, k = x.shape
    n = w.shape[-1]
    assert n % tn == 0 and t % MM_TM == 0
    return pl.pallas_call(
        _mm_kernel,
        grid=(n // tn, t // MM_TM),
        in_specs=[pl.BlockSpec((MM_TM, k), lambda j, i: (i, 0)), _w_spec(w, wl, k, tn)],
        out_specs=pl.BlockSpec((MM_TM, tn), lambda j, i: (i, j)),
        out_shape=jax.ShapeDtypeStruct((t, n), out_dtype),
        scratch_shapes=[pltpu.VMEM((k, tn), BF16)],
        compiler_params=_cparams(("arbitrary", "arbitrary")),
        name="matmul",
    )(x, w)


def matmul_residual(x, w, wl, res, mods, li, gate_comp, tn=512):
    t, k = x.shape
    n = w.shape[-1]
    assert n % tn == 0 and t % MM_TM == 0
    gate_spec = pl.BlockSpec((None, None, 1, tn),
                             lambda j, i: (li, _mod_row(i, MM_TM) * N_MOD + gate_comp, 0, j))
    return pl.pallas_call(
        _mm_res_kernel,
        grid=(n // tn, t // MM_TM),
        in_specs=[pl.BlockSpec((MM_TM, k), lambda j, i: (i, 0)), _w_spec(w, wl, k, tn),
                  pl.BlockSpec((MM_TM, tn), lambda j, i: (i, j)), gate_spec],
        out_specs=pl.BlockSpec((MM_TM, tn), lambda j, i: (i, j)),
        out_shape=jax.ShapeDtypeStruct((t, n), F32),
        scratch_shapes=[pltpu.VMEM((k, tn), BF16)],
        compiler_params=_cparams(("arbitrary", "arbitrary")),
        name="matmul_residual",
    )(x, w, res, mods)


MOE_BLK = 256
MOE_SB_BLKS = 8
MOE_RB = MOE_BLK * MOE_SB_BLKS
MOE_TH = 256
MOE_NJ = D_EXPERT // MOE_TH
D_PACK = D_MODEL // 2


def _moe_nblk(t):
    return pl.cdiv(t * TOP_K, MOE_BLK) + N_EXPERTS


def _moe_nsb(t):
    return pl.cdiv(t * TOP_K, MOE_RB) + N_EXPERTS


def moe_dispatch(top_i):
    t = top_i.shape[1]
    moe_rows, moe_nsb = _moe_nblk(t) * MOE_BLK, _moe_nsb(t)
    flat_e = top_i.T.reshape(-1)
    onehot = (flat_e[:, None] == jnp.arange(N_EXPERTS, dtype=jnp.int32)[None, :]).astype(jnp.int32)
    csum = jnp.cumsum(onehot, axis=0)
    rank = jnp.take_along_axis(csum, flat_e[:, None], axis=1)[:, 0] - 1
    counts = csum[-1]
    nblk = (counts + MOE_BLK - 1) // MOE_BLK
    blk_end = jnp.cumsum(nblk)
    blk_start = blk_end - nblk
    pos = blk_start[flat_e] * MOE_BLK + rank
    tok_buf = jnp.zeros((moe_rows,), jnp.int32).at[pos].set(jnp.arange(t * TOP_K, dtype=jnp.int32) // TOP_K)
    n_blk = blk_end[-1]
    nsb_e = (nblk + MOE_SB_BLKS - 1) // MOE_SB_BLKS
    sb_end = jnp.cumsum(nsb_e)
    sb_start = sb_end - nsb_e
    n_sb = sb_end[-1]
    sb = jnp.arange(moe_nsb, dtype=jnp.int32)
    e_of = jnp.minimum(jnp.searchsorted(sb_end, sb, side='right'), N_EXPERTS - 1).astype(jnp.int32)
    local = sb - sb_start[e_of]
    valid = sb < n_sb
    e_last = e_of[jnp.maximum(n_sb - 1, 0)]
    sb_e = jnp.where(valid, e_of, e_last).astype(jnp.int32)
    sb_blk0 = jnp.where(valid, blk_start[e_of] + local * MOE_SB_BLKS, 0).astype(jnp.int32)
    sb_nb = jnp.where(valid, jnp.clip(nblk[e_of] - local * MOE_SB_BLKS, 0, MOE_SB_BLKS), 0).astype(jnp.int32)
    return dict(pos=pos.astype(jnp.int32), tok_buf=tok_buf, n_blk=n_blk.reshape(1).astype(jnp.int32),
                sb_e=sb_e, sb_blk0=sb_blk0, sb_nb=sb_nb, n_sb=n_sb.reshape(1).astype(jnp.int32))


def _pack_bf16_pairs(h):
    half = h.shape[-1] // 2
    lo = lax.bitcast_convert_type(h[:, :half].astype(BF16).astype(F32), jnp.uint32)
    hi = lax.bitcast_convert_type(h[:, half:].astype(BF16).astype(F32), jnp.uint32)
    return (hi & jnp.uint32(0xFFFF0000)) | (lo >> 16)


def _unpack_bf16_pairs(p):
    lo = lax.bitcast_convert_type(p << 16, F32).astype(BF16)
    hi = lax.bitcast_convert_type(p & jnp.uint32(0xFFFF0000), F32).astype(BF16)
    return lo, hi


def _gather_kernel(nblk_ref, tok_ref, h_ref, xs_ref, sem):
    b = pl.program_id(0)

    def row_copy(r):
        return pltpu.make_async_copy(h_ref.at[pl.ds(tok_ref[0, r], 1)],
                                     xs_ref.at[pl.ds(b * MOE_BLK + r, 1)], sem)

    @pl.when(b < nblk_ref[0])
    def _():
        def issue(r, c):
            row_copy(r).start()
            return c

        lax.fori_loop(0, MOE_BLK, issue, 0)

        def drain(r, c):
            row_copy(r).wait()
            return c

        lax.fori_loop(0, MOE_BLK, drain, 0)


def moe_gather(h_packed, tok_buf, n_blk):
    nblk_max = tok_buf.shape[0] // MOE_BLK
    return pl.pallas_call(
        _gather_kernel,
        grid_spec=pltpu.PrefetchScalarGridSpec(
            num_scalar_prefetch=1,
            grid=(nblk_max,),
            in_specs=[
                pl.BlockSpec((None, 1, MOE_BLK), lambda b, n: (b, 0, 0), memory_space=pltpu.SMEM),
                pl.BlockSpec(memory_space=pl.ANY),
            ],
            out_specs=pl.BlockSpec(memory_space=pl.ANY),
            scratch_shapes=[pltpu.SemaphoreType.DMA(())],
        ),
        out_shape=jax.ShapeDtypeStruct((nblk_max * MOE_BLK, D_PACK), jnp.uint32),
        compiler_params=_cparams(("arbitrary",)),
        name="moe_gather",
    )(n_blk, tok_buf.reshape(nblk_max, 1, MOE_BLK), h_packed)


def _moe_kernel(sbe_ref, sbb_ref, sbn_ref, nsb_ref,
                xs_ref, wg_ref, wu_ref, bg_ref, bu_ref, wo_ref, bo_ref,
                ys_ref,
                xp_buf, xb_buf, y_acc, wg_b, wu_b, wo_b, sem_in, sem_out):
    sb = pl.program_id(0)
    j = pl.program_id(1)
    nb = sbn_ref[sb]
    row0 = sbb_ref[sb] * MOE_BLK

    def rows(rb):
        return pl.ds(pl.multiple_of(rb * MOE_BLK, MOE_BLK), MOE_BLK)

    def in_copy(rb):
        return pltpu.make_async_copy(xs_ref.at[pl.ds(row0 + rb * MOE_BLK, MOE_BLK)], xp_buf.at[rows(rb)], sem_in)

    def out_copy(rb):
        return pltpu.make_async_copy(y_acc.at[rows(rb)], ys_ref.at[pl.ds(row0 + rb * MOE_BLK, MOE_BLK)], sem_out)

    def for_blocks(fn):
        def body(rb, c):
            fn(rb)
            return c
        lax.fori_loop(0, nb, body, 0)

    @pl.when(jnp.logical_and(nb > 0, j == 0))
    def _():
        for_blocks(lambda rb: in_copy(rb).start())
        y_acc[...] = jnp.broadcast_to(bo_ref[...], y_acc.shape)
        for_blocks(lambda rb: in_copy(rb).wait())

        def unpack(rb):
            lo, hi = _unpack_bf16_pairs(xp_buf[rows(rb), :])
            xb_buf[rows(rb), :D_PACK] = lo
            xb_buf[rows(rb), D_PACK:] = hi

        for_blocks(unpack)

    def ffn_block(rb, last):
        x = xb_buf[rows(rb), :]
        hg = jnp.dot(x, wg_b[...], preferred_element_type=F32) + bg_ref[...]
        hu = jnp.dot(x, wu_b[...], preferred_element_type=F32) + bu_ref[...]
        g = jnp.minimum(hg, SWIGLU_LIMIT)
        u = jnp.clip(hu, -SWIGLU_LIMIT, SWIGLU_LIMIT)
        a = (u + 1.0) * g * jax.nn.sigmoid(SWIGLU_ALPHA * g)
        y_acc[rows(rb), :] += jnp.dot(a.astype(BF16), wo_b[...], preferred_element_type=F32)
        if last:
            out_copy(rb).start()

    @pl.when(nb > 0)
    def _():
        wg_b[...] = wg_ref[...].astype(BF16)
        wu_b[...] = wu_ref[...].astype(BF16)
        wo_b[...] = wo_ref[...].astype(BF16)

    @pl.when(jnp.logical_and(nb > 0, j < MOE_NJ - 1))
    def _():
        for_blocks(lambda rb: ffn_block(rb, False))

    @pl.when(jnp.logical_and(nb > 0, j == MOE_NJ - 1))
    def _():
        for_blocks(lambda rb: ffn_block(rb, True))
        for_blocks(lambda rb: out_copy(rb).wait())


def moe_experts(xs, tables, w_in, b_in, w_out, b_out, li):
    def live_j(sb, j, nsb):
        return jnp.where(sb < nsb[0], j, MOE_NJ - 1)

    w_in_spec = lambda half: pl.BlockSpec(
        (None, None, D_MODEL, MOE_TH),
        lambda sb, j, sbe, sbb, sbn, nsb: (li, sbe[sb], 0, half * MOE_NJ + live_j(sb, j, nsb)))
    b_in_spec = lambda half: pl.BlockSpec(
        (None, None, 1, MOE_TH),
        lambda sb, j, sbe, sbb, sbn, nsb: (li, sbe[sb], 0, half * MOE_NJ + live_j(sb, j, nsb)))
    return pl.pallas_call(
        _moe_kernel,
        grid_spec=pltpu.PrefetchScalarGridSpec(
            num_scalar_prefetch=4,
            grid=(tables["sb_e"].shape[0], MOE_NJ),
            in_specs=[
                pl.BlockSpec(memory_space=pl.ANY),
                w_in_spec(0), w_in_spec(1), b_in_spec(0), b_in_spec(1),
                pl.BlockSpec((None, None, MOE_TH, D_MODEL),
                             lambda sb, j, sbe, sbb, sbn, nsb: (li, sbe[sb], live_j(sb, j, nsb), 0)),
                pl.BlockSpec((None, None, 1, D_MODEL), lambda sb, j, sbe, sbb, sbn, nsb: (li, sbe[sb], 0, 0)),
            ],
            out_specs=pl.BlockSpec(memory_space=pl.ANY),
            scratch_shapes=[
                pltpu.VMEM((MOE_RB, D_PACK), jnp.uint32),
                pltpu.VMEM((MOE_RB, D_MODEL), BF16),
                pltpu.VMEM((MOE_RB, D_MODEL), F32),
                pltpu.VMEM((D_MODEL, MOE_TH), BF16),
                pltpu.VMEM((D_MODEL, MOE_TH), BF16),
                pltpu.VMEM((MOE_TH, D_MODEL), BF16),
                pltpu.SemaphoreType.DMA(()),
                pltpu.SemaphoreType.DMA(()),
            ],
        ),
        out_shape=jax.ShapeDtypeStruct((xs.shape[0], D_MODEL), F32),
        compiler_params=_cparams(("arbitrary", "arbitrary")),
        name="moe_experts",
    )(tables["sb_e"], tables["sb_blk0"], tables["sb_nb"], tables["n_sb"],
      xs, w_in, w_in, b_in.reshape(-1, N_EXPERTS, 1, 2 * D_EXPERT), b_in.reshape(-1, N_EXPERTS, 1, 2 * D_EXPERT),
      w_out, b_out.reshape(-1, N_EXPERTS, 1, D_MODEL))


CMB_TM = 128


def _combine_kernel(pos_ref, ys_ref, x_ref, tg_ref, mg_ref, o_ref, buf, sem):
    def row_copy(n):
        k = n // CMB_TM
        r = n - k * CMB_TM
        return pltpu.make_async_copy(ys_ref.at[pl.ds(pos_ref[0, n], 1)], buf.at[k, pl.ds(r, 1)], sem)

    def issue(n, c):
        row_copy(n).start()
        return c

    lax.fori_loop(0, TOP_K * CMB_TM, issue, 0)

    def drain(n, c):
        row_copy(n).wait()
        return c

    lax.fori_loop(0, TOP_K * CMB_TM, drain, 0)
    tg = tg_ref[...]
    y = tg[:, 0:1] * buf[0]
    for k in range(1, TOP_K):
        y = y + tg[:, k:k + 1] * buf[k]
    o_ref[...] = x_ref[...] + mg_ref[...] * y


def moe_combine(ys, pos, gates_t, x, mods, li):
    t = x.shape[0]
    nt = t // CMB_TM
    pos_tiles = pos.reshape(nt, CMB_TM, TOP_K).transpose(0, 2, 1).reshape(nt, 1, TOP_K * CMB_TM)
    return pl.pallas_call(
        _combine_kernel,
        grid=(nt,),
        in_specs=[
            pl.BlockSpec((None, 1, TOP_K * CMB_TM), lambda i: (i, 0, 0), memory_space=pltpu.SMEM),
            pl.BlockSpec(memory_space=pl.ANY),
            pl.BlockSpec((CMB_TM, D_MODEL), lambda i: (i, 0)),
            pl.BlockSpec((CMB_TM, TOP_K), lambda i: (i, 0)),
            _mod_spec(li, 5, CMB_TM),
        ],
        out_specs=pl.BlockSpec((CMB_TM, D_MODEL), lambda i: (i, 0)),
        out_shape=jax.ShapeDtypeStruct((t, D_MODEL), F32),
        scratch_shapes=[pltpu.VMEM((TOP_K, CMB_TM, D_MODEL), F32), pltpu.SemaphoreType.DMA(())],
        compiler_params=_cparams(("arbitrary",)),
        name="moe_combine",
    )(pos_tiles, ys, x, gates_t, mods)


def moe_layer(x, li, mods, norm_ffn, w_router_t, b_router, w_in, b_in, w_out, b_out):
    hp, top_i, top_g = norm_router(x, norm_ffn, mods, li, w_router_t, b_router)
    tables = moe_dispatch(top_i)
    xs = moe_gather(hp, tables["tok_buf"], tables["n_blk"])
    ys = moe_experts(xs, tables, w_in, b_in, w_out, b_out, li)
    return moe_combine(ys, tables["pos"], top_g.T, x, mods, li)


AT_NH = AT_HEADS + 2 * AT_KV_HEADS
AT_SCALE = AT_HEAD_DIM ** -0.5


def _rope_tables(length):
    pos = np.arange(length)
    half, quarter = AT_HEAD_DIM // 2, AT_HEAD_DIM // 4
    inv = ROPE_THETA ** (-np.arange(0, half, 2, dtype=np.float64) / half)
    j = np.arange(AT_HEAD_DIM)
    p = np.where(j[None, :] < half, (pos // GRID_W)[:, None], (pos % GRID_W)[:, None]).astype(np.float64)
    ang = p * inv[(j % half) % quarter][None, :]
    first = (j % half) < quarter
    sin_signed = np.where(first[None, :], -np.sin(ang), np.sin(ang))
    partner = np.where(first, j + quarter, j - quarter)
    perm = np.zeros((AT_HEAD_DIM, AT_HEAD_DIM), np.float32)
    perm[partner, j] = 1.0
    return jnp.asarray(np.cos(ang), F32), jnp.asarray(sin_signed, F32), jnp.asarray(perm)


def _rope_kernel(x_ref, cos_ref, sin_ref, perm_ref, o_ref):
    x = x_ref[...]
    xp = jnp.dot(x, perm_ref[...], precision=HIGHEST, preferred_element_type=F32)
    o_ref[...] = x * cos_ref[...] + xp * sin_ref[...]


def rope_heads(xh, n_heads):
    b, _, length, dh = xh.shape
    cos, sin, perm = _rope_tables(length)
    return pl.pallas_call(
        _rope_kernel,
        grid=(b, n_heads),
        in_specs=[
            pl.BlockSpec((None, None, length, dh), lambda i, h: (i, h, 0, 0)),
            pl.BlockSpec((length, dh), lambda i, h: (0, 0)),
            pl.BlockSpec((length, dh), lambda i, h: (0, 0)),
            pl.BlockSpec((dh, dh), lambda i, h: (0, 0)),
        ],
        out_specs=pl.BlockSpec((None, None, length, dh), lambda i, h: (i, h, 0, 0)),
        out_shape=jax.ShapeDtypeStruct((b, n_heads, length, dh), F32),
        compiler_params=_cparams(("arbitrary", "arbitrary")),
        name="rope_heads",
    )(xh, cos, sin, perm)


def _nt_dot(a, b):
    return lax.dot_general(a, b, (((1,), (1,)), ((), ())), preferred_element_type=F32)


def _attn_ctx_kernel(sink_ref, q_ref, k_ref, v_ref, o_ref):
    g = pl.program_id(1)
    k = k_ref[...].astype(BF16)
    v = v_ref[...].astype(BF16)
    for r in range(AT_REP):
        q = (q_ref[r] * AT_SCALE).astype(BF16)
        s = _nt_dot(q, k)
        sk = sink_ref[g * AT_REP + r]
        m = jnp.maximum(jnp.max(s, axis=-1, keepdims=True), sk)
        p = jnp.exp(s - m)
        den = jnp.sum(p, axis=-1, keepdims=True) + jnp.exp(sk - m)
        o = jnp.dot(p.astype(BF16), v, preferred_element_type=F32) / den
        o_ref[r] = o.astype(o_ref.dtype)


def attn_context(qkvh, sink):
    b, _, length, dh = qkvh.shape
    return pl.pallas_call(
        _attn_ctx_kernel,
        grid=(b, AT_KV_HEADS),
        in_specs=[
            pl.BlockSpec(memory_space=pltpu.SMEM),
            pl.BlockSpec((None, AT_REP, length, dh), lambda i, g: (i, g, 0, 0)),
            pl.BlockSpec((None, None, length, dh), lambda i, g: (i, AT_HEADS + g, 0, 0)),
            pl.BlockSpec((None, None, length, dh), lambda i, g: (i, AT_HEADS + AT_KV_HEADS + g, 0, 0)),
        ],
        out_specs=pl.BlockSpec((None, AT_REP, length, dh), lambda i, g: (i, g, 0, 0)),
        out_shape=jax.ShapeDtypeStruct((b, AT_HEADS, length, dh), BF16),
        compiler_params=_cparams(("arbitrary", "arbitrary")),
        name="attn_context",
    )(sink, qkvh, qkvh, qkvh)


def _attn_lat_kernel(sink_ref, q_ref, k0_ref, k1_ref, k2_ref, v0_ref, v1_ref, v2_ref, kc_ref, vc_ref, o_ref, *, length):
    g = pl.program_id(1)
    i = pl.program_id(2)
    kb = jnp.concatenate([k0_ref[...], k1_ref[...], k2_ref[...]], axis=0).astype(BF16)
    vb = jnp.concatenate([v0_ref[...], v1_ref[...], v2_ref[...]], axis=0).astype(BF16)
    kc = kc_ref[...].astype(BF16)
    vc = vc_ref[...].astype(BF16)
    qpos = i * AT_BLOCK + lax.broadcasted_iota(jnp.int32, (AT_BLOCK, 3 * AT_BLOCK), 0)
    kpos = (i - 1) * AT_BLOCK + lax.broadcasted_iota(jnp.int32, (AT_BLOCK, 3 * AT_BLOCK), 1)
    allowed = (jnp.abs(qpos - kpos) <= AT_WINDOW) & (kpos >= 0) & (kpos < length)
    for r in range(AT_REP):
        q = (q_ref[r] * AT_SCALE).astype(BF16)
        s1 = jnp.where(allowed, _nt_dot(q, kb), -jnp.inf)
        s2 = _nt_dot(q, kc)
        sk = sink_ref[g * AT_REP + r]
        m = jnp.maximum(jnp.maximum(jnp.max(s1, axis=-1, keepdims=True), jnp.max(s2, axis=-1, keepdims=True)), sk)
        p1 = jnp.exp(s1 - m)
        p2 = jnp.exp(s2 - m)
        den = jnp.sum(p1, axis=-1, keepdims=True) + jnp.sum(p2, axis=-1, keepdims=True) + jnp.exp(sk - m)
        o = (jnp.dot(p1.astype(BF16), vb, preferred_element_type=F32)
             + jnp.dot(p2.astype(BF16), vc, preferred_element_type=F32)) / den
        o_ref[r] = o.astype(o_ref.dtype)


def attn_latent(qk_rope, qkvh, k_ctx, v_ctx, sink):
    b, _, length, dh = qkvh.shape
    nblk = length // AT_BLOCK
    lc = k_ctx.shape[2]
    band = lambda head0, shift: pl.BlockSpec(
        (None, None, AT_BLOCK, dh), lambda bi, g, i: (bi, head0 + g, jnp.clip(i + shift, 0, nblk - 1), 0))
    ctx = pl.BlockSpec((None, None, lc, dh), lambda bi, g, i: (bi, g, 0, 0))
    return pl.pallas_call(
        functools.partial(_attn_lat_kernel, length=length),
        grid=(b, AT_KV_HEADS, nblk),
        in_specs=[
            pl.BlockSpec(memory_space=pltpu.SMEM),
            pl.BlockSpec((None, AT_REP, AT_BLOCK, dh), lambda bi, g, i: (bi, g, i, 0)),
            band(AT_HEADS, -1), band(AT_HEADS, 0), band(AT_HEADS, 1),
            band(AT_HEADS + AT_KV_HEADS, -1), band(AT_HEADS + AT_KV_HEADS, 0), band(AT_HEADS + AT_KV_HEADS, 1),
            ctx, ctx,
        ],
        out_specs=pl.BlockSpec((None, AT_REP, AT_BLOCK, dh), lambda bi, g, i: (bi, g, i, 0)),
        out_shape=jax.ShapeDtypeStruct((b, AT_HEADS, length, dh), BF16),
        compiler_params=_cparams(("arbitrary", "arbitrary", "arbitrary")),
        name="attn_latent",
    )(sink, qk_rope, qk_rope, qk_rope, qk_rope, qkvh, qkvh, qkvh, k_ctx, v_ctx)


def attention_mixer(h, w_qkv, wl, sink, cache_k, cache_v):
    qkv = matmul(h, w_qkv, wl, tn=512)
    nkv = AT_KV_HEADS * AT_HEAD_DIM
    new_k = qkv[:T_PROMPT, D_MODEL:D_MODEL + nkv].reshape(BATCH, SEQ, AT_KV_HEADS, AT_HEAD_DIM)
    new_v = qkv[:T_PROMPT, D_MODEL + nkv:].reshape(BATCH, SEQ, AT_KV_HEADS, AT_HEAD_DIM)
    qkv_p = qkv[:T_PROMPT].reshape(BATCH, SEQ, AT_NH, AT_HEAD_DIM).transpose(0, 2, 1, 3)
    qkv_s = qkv[T_PROMPT:].reshape(DEC_BATCH, DEC_SEQ, AT_NH, AT_HEAD_DIM).transpose(0, 2, 1, 3)
    o_p = attn_context(qkv_p, sink)
    qk_rope = rope_heads(qkv_s, AT_HEADS + AT_KV_HEADS)
    o_s = attn_latent(qk_rope, qkv_s, cache_k.transpose(0, 2, 1, 3), cache_v.transpose(0, 2, 1, 3), sink)
    o = jnp.concatenate([o_p.transpose(0, 2, 1, 3).reshape(T_PROMPT, D_MODEL),
                         o_s.transpose(0, 2, 1, 3).reshape(T_SAMPLE, D_MODEL)], axis=0)
    return o, new_k, new_v


HY_FEAT_PAD = 128


def _dft_matrices(length):
    n = np.arange(length, dtype=np.float64)
    ang = np.pi * np.outer(n, n) / length
    alt = np.where(n % 2 == 0, 1.0, -1.0)
    fwd_im = -np.sin(ang)
    fwd_im[0, :] = alt
    fwd = np.concatenate([np.cos(ang), fwd_im], axis=0)
    wk = np.where(n == 0, 1.0, 2.0)[None, :] / (2.0 * length)
    inv_im = -wk * np.sin(ang)
    inv_im[:, 0] = alt / (2.0 * length)
    inv = np.concatenate([wk * np.cos(ang), inv_im], axis=1)
    return jnp.asarray(fwd, F32).astype(BF16), jnp.asarray(inv, F32).astype(BF16)


def _hyena_features(length):
    t = np.linspace(0.0, 1.0, length)[:, None]
    om = 2.0 * np.pi * np.arange(length)[:, None] / length
    f = np.linspace(1e-4, HY_BANDS - 1, HY_BANDS)[None, :]
    z = np.concatenate([t, np.cos(f * om), -np.sin(f * om)], axis=-1)
    return jnp.asarray(np.pad(z, ((0, 0), (0, HY_FEAT_PAD - HY_EMB_DIM))), F32)


def _hyena_deltas():
    max_decay = math.log(HY_DECAY_TARGET) / HY_FAST_DECAY
    min_decay = math.log(HY_DECAY_TARGET) / HY_SLOW_DECAY
    return jnp.asarray(np.abs(np.linspace(min_decay, max_decay, D_MODEL)), F32).reshape(1, D_MODEL)


def _hy_filter_kernel(z_ref, w1_ref, b1_ref, wh_ref, bh_ref, fr_ref, wo00, wo01, wo10, wo11, dl_ref, fwd_ref, k_ref,
                      *, length):
    hp = functools.partial(jnp.dot, precision=HIGHEST, preferred_element_type=F32)
    h = jnp.sin(fr_ref[0] * (hp(z_ref[...], w1_ref[...]) + b1_ref[...]))
    for s in range(HY_N_SIN - 1):
        h = jnp.sin(fr_ref[s + 1] * (hp(h, wh_ref[s]) + bh_ref[s]))
    td = dl_ref.shape[-1]
    row = lax.broadcasted_iota(jnp.int32, (length, td), 0)
    t = row.astype(F32) * (1.0 / (length - 1))
    decay = jnp.exp(-t * dl_ref[...])
    fwd = fwd_ref[...]
    for o, (wf, wb) in enumerate(((wo00, wo01), (wo10, wo11))):
        hf = hp(h, wf[...]) * decay
        hb = jnp.where(row == 0, 0.0, hp(h, wb[...]) * decay)
        ks = jnp.dot(fwd, (hf + hb).astype(BF16), preferred_element_type=F32)
        kd = jnp.dot(fwd[length:], (hf - hb).astype(BF16), preferred_element_type=F32)
        k_ref[o, :length, :] = ks[:length]
        k_ref[o, length:, :] = jnp.where(row == 0, ks[length:length + 1], kd)


def hyena_filter_spectrum(length, wl, w1, b1, w_hid, b_hid, freq, w_filt_out, td=512):
    fwd, _ = _dft_matrices(length)
    nj = D_MODEL // td
    w1p = jnp.pad(w1, ((0, 0), (0, HY_FEAT_PAD - HY_EMB_DIM), (0, 0)))
    n_hy = w1.shape[0]
    wout_spec = lambda c: pl.BlockSpec((None, HY_FILTER_W, td), lambda j: (wl, 0, c * nj + j))
    whole = lambda shape: pl.BlockSpec((None,) + shape, lambda j: (wl,) + (0,) * len(shape))
    return pl.pallas_call(
        functools.partial(_hy_filter_kernel, length=length),
        grid=(nj,),
        in_specs=[
            pl.BlockSpec((length, HY_FEAT_PAD), lambda j: (0, 0)),
            whole((HY_FEAT_PAD, HY_FILTER_W)), whole((1, HY_FILTER_W)),
            whole((HY_N_SIN - 1, HY_FILTER_W, HY_FILTER_W)), whole((HY_N_SIN - 1, 1, HY_FILTER_W)),
            whole((HY_N_SIN, 1, HY_FILTER_W)),
            wout_spec(0), wout_spec(1), wout_spec(2), wout_spec(3),
            pl.BlockSpec((1, td), lambda j: (0, j)),
            pl.BlockSpec((2 * length, length), lambda j: (0, 0)),
        ],
        out_specs=pl.BlockSpec((HY_ORDER, 2 * length, td), lambda j: (0, 0, j)),
        out_shape=jax.ShapeDtypeStruct((HY_ORDER, 2 * length, D_MODEL), F32),
        compiler_params=_cparams(("arbitrary",)),
        name="hyena_filter",
    )(_hyena_features(length), w1p, b1.reshape(n_hy, 1, HY_FILTER_W), w_hid,
      b_hid.reshape(n_hy, HY_N_SIN - 1, 1, HY_FILTER_W), freq.reshape(n_hy, HY_N_SIN, 1, HY_FILTER_W),
      w_filt_out, w_filt_out, w_filt_out, w_filt_out, _hyena_deltas(), fwd)


def _hy_conv_kernel(pv_ref, p1_ref, p2_ref, wv_ref, w1_ref, w2_ref, k_ref, skip_ref, fwd_ref, inv_ref, o_ref, *, length):
    td = o_ref.shape[-1]
    row = lax.broadcasted_iota(jnp.int32, (length, td), 0)

    def short_conv(p_ref, w_ref):
        p = p_ref[...]
        w = w_ref[...]
        prev = jnp.where(row == 0, 0.0, pltpu.roll(p, 1, 0))
        nxt = jnp.where(row == length - 1, 0.0, pltpu.roll(p, length - 1, 0))
        return prev * w[0:1] + p * w[1:2] + nxt * w[2:3]

    z = short_conv(pv_ref, wv_ref)
    gates = (short_conv(p1_ref, w1_ref), short_conv(p2_ref, w2_ref))
    fwd = fwd_ref[...]
    inv = inv_ref[...]
    for o in range(HY_ORDER):
        zf = jnp.dot(fwd, z.astype(BF16), preferred_element_type=F32)
        zr, zi = zf[:length], zf[length:]
        kr, ki = k_ref[o, :length, :], k_ref[o, length:, :]
        ii = zi * ki
        pr = zr * kr - jnp.where(row == 0, 0.0, ii)
        pi = jnp.where(row == 0, ii, zr * ki + zi * kr)
        conv = jnp.dot(inv, jnp.concatenate([pr, pi], axis=0).astype(BF16), preferred_element_type=F32)
        z = gates[o] * (conv + z * skip_ref[o:o + 1, :])
    o_ref[...] = z.astype(o_ref.dtype)


def hyena_conv(p, row0, n_seq, length, spectrum, w_short, wl, skip, td):
    fwd, inv = _dft_matrices(length)
    nj = D_MODEL // td
    b0 = row0 // length
    p_spec = lambda c: pl.BlockSpec((length, td), lambda b, j: (b0 + b, c * nj + j))
    w_spec = lambda c: pl.BlockSpec((None, HY_SHORT_W, td), lambda b, j: (wl, 0, c * nj + j))
    return pl.pallas_call(
        functools.partial(_hy_conv_kernel, length=length),
        grid=(n_seq, nj),
        in_specs=[
            p_spec(0), p_spec(1), p_spec(2), w_spec(0), w_spec(1), w_spec(2),
            pl.BlockSpec((HY_ORDER, 2 * length, td), lambda b, j: (0, 0, j)),
            pl.BlockSpec((None, HY_ORDER, td), lambda b, j: (wl, 0, j)),
            pl.BlockSpec((2 * length, length), lambda b, j: (0, 0)),
            pl.BlockSpec((length, 2 * length), lambda b, j: (0, 0)),
        ],
        out_specs=pl.BlockSpec((length, td), lambda b, j: (b, j)),
        out_shape=jax.ShapeDtypeStruct((n_seq * length, D_MODEL), BF16),
        compiler_params=_cparams(("arbitrary", "arbitrary")),
        name="hyena_conv",
    )(p, p, p, w_short, w_short, w_short, spectrum, skip, fwd, inv)


def hyena_mixer(h, wl, w_in, w_short, w1, b1, w_hid, b_hid, freq, w_filt_out, skip):
    p = matmul(h, w_in, wl)
    filt = (w1, b1, w_hid, b_hid, freq, w_filt_out)
    z_p = hyena_conv(p, 0, BATCH, SEQ, hyena_filter_spectrum(SEQ, wl, *filt), w_short, wl, skip, td=D_MODEL)
    z_s = hyena_conv(p, T_PROMPT, DEC_BATCH, DEC_SEQ, hyena_filter_spectrum(DEC_SEQ, wl, *filt), w_short, wl, skip, td=512)
    return jnp.concatenate([z_p, z_s], axis=0)


SSD_Q = 128
SSD_R = SSD_HEADS // SSD_GROUPS
SSD_GW = SSD_R * SSD_HEADDIM


def _shift_rows(x, s, row, length):
    if s == 0:
        return x
    rolled = pltpu.roll(x, (-s) % length, 0)
    ok = (row + s >= 0) & (row + s < length)
    return jnp.where(ok, rolled, 0.0)


def _silu(x):
    return x * jax.nn.sigmoid(x)


def _softplus(x):
    return jnp.maximum(x, 0.0) + jnp.log1p(jnp.exp(-jnp.abs(x)))


def _ssd_kernel(*refs, length, has_init, emit_state):
    (z_ref, x_ref, b_ref, c_ref, wx_ref, wb_ref, wc_ref, bx_ref, bb_ref, bc_ref,
     dtc_ref, dtr_ref, dbc_ref, dbr_ref, alc_ref, alr_ref, dsk_ref, nw_ref, exp_ref) = refs[:19]
    pos = 19
    if has_init:
        s0_ref = refs[pos]
        pos += 1
    o_ref = refs[pos]
    pos += 1
    if emit_state:
        so_ref = refs[pos]
        pos += 1
    xc_s, bc_s, cc_s, y_s, st_s = refs[pos:pos + 5]

    hp = functools.partial(jnp.dot, precision=HIGHEST, preferred_element_type=F32)

    def conv_silu(p_ref, w_ref, bias_ref):
        p = p_ref[...]
        w = w_ref[...]
        row = lax.broadcasted_iota(jnp.int32, p.shape, 0)
        acc = bias_ref[...] + _shift_rows(p, -2, row, length) * w[0:1]
        for k in range(1, SSD_CONV_W):
            acc = acc + _shift_rows(p, k - 2, row, length) * w[k:k + 1]
        return _silu(acc)

    xc_s[...] = conv_silu(x_ref, wx_ref, bx_ref)
    bc_s[...] = conv_silu(b_ref, wb_ref, bb_ref)
    cc_s[...] = conv_silu(c_ref, wc_ref, bc_ref)
    y_s[...] = xc_s[...] * dsk_ref[...]

    expand = exp_ref[...]
    ti = lax.broadcasted_iota(jnp.int32, (SSD_Q, SSD_Q), 0)
    si = lax.broadcasted_iota(jnp.int32, (SSD_Q, SSD_Q), 1)
    lane = lax.broadcasted_iota(jnp.int32, (SSD_Q, 2 * SSD_HEADDIM), 1)
    n_chunks = length // SSD_Q

    for d in range(2):
        causal = (si <= ti) if d == 0 else (si >= ti)
        tri = causal.astype(F32)
        tri_t = ((ti <= si) if d == 0 else (ti >= si)).astype(F32)
        a_col = -jnp.exp(alc_ref[d])
        a_row = -jnp.exp(alr_ref[d])
        if has_init:
            st_s[...] = s0_ref[d]
        else:
            st_s[...] = jnp.zeros_like(st_s)

        def chunk(ci, carry, d=d, causal=causal, tri=tri, tri_t=tri_t, a_col=a_col, a_row=a_row):
            c = ci if d == 0 else n_chunks - 1 - ci
            r0 = pl.multiple_of(c * SSD_Q, SSD_Q)
            rows = pl.ds(r0, SSD_Q)
            dt_c = _softplus(dtc_ref[d, rows, :] + dbc_ref[d])
            dt_r = _softplus(dtr_ref[d, :, rows] + dbr_ref[d])
            acs = hp(tri, dt_c * a_col)
            acs_r = hp(dt_r * a_row, tri_t)
            end = acs[SSD_Q - 1:SSD_Q, :] if d == 0 else acs[0:1, :]
            e_acs = hp(jnp.exp(acs), expand)
            w_in = hp(jnp.exp(end - acs) * dt_c, expand)
            e_end = e_acs[SSD_Q - 1:SSD_Q, :] if d == 0 else e_acs[0:1, :]
            xq = xc_s[rows, :]
            bq = bc_s[rows, :]
            cq = cc_s[rows, :].astype(BF16)
            cb = _nt_dot(cq, bq.astype(BF16))
            st = st_s[...]
            y = e_acs * jnp.dot(cq, st.astype(BF16), preferred_element_type=F32)
            xb = xq.astype(BF16)
            pieces = []
            for pr in range(SSD_R // 2):
                xp = xb[:, pr * 2 * SSD_HEADDIM:(pr + 1) * 2 * SSD_HEADDIM]
                acc = None
                for sub in range(2):
                    r = 2 * pr + sub
                    seg = acs[:, r:r + 1] - acs_r[r:r + 1, :]
                    w = jnp.where(causal, jnp.exp(jnp.where(causal, seg, 0.0)) * cb * dt_r[r:r + 1, :], 0.0)
                    in_head = (lane < SSD_HEADDIM) if sub == 0 else (lane >= SSD_HEADDIM)
                    part = jnp.dot(w.astype(BF16), jnp.where(in_head, xp, jnp.zeros_like(xp)),
                                   preferred_element_type=F32)
                    acc = part if acc is None else acc + part
                pieces.append(acc)
            y_s[rows, :] += y + jnp.concatenate(pieces, axis=1)
            upd = lax.dot_general(bq.astype(BF16), (xq * w_in).astype(BF16), (((0,), (0,)), ((), ())),
                                  preferred_element_type=F32)
            st_s[...] = e_end * st + upd
            return carry

        lax.fori_loop(0, n_chunks, chunk, 0)
        if emit_state:
            so_ref[d] = st_s[...]

    y = y_s[...] * _silu(z_ref[...])
    y = y * lax.rsqrt(jnp.mean(y * y, axis=-1, keepdims=True) + EPS) * nw_ref[...]
    o_ref[...] = y.astype(o_ref.dtype)


def ssd_scan(proj, row0, n_seq, length, dt_col, dt_row, wl, conv_w, conv_b, dtb_col, dtb_row, al_col, al_row,
             d_skip_ch, norm_w, s0, emit_state):
    b0 = row0 // length
    gw_blk = SSD_INNER // SSD_GW
    n_blk = SSD_GN // SSD_STATE
    seq_cols = lambda width, base: pl.BlockSpec((length, width), lambda b, g: (b0 + b, base + g))
    par = lambda rows_, width, base: pl.BlockSpec((None, rows_, width), lambda b, g: (wl, 0, base + g))
    in_specs = [
        seq_cols(SSD_GW, 0),
        seq_cols(SSD_GW, gw_blk),
        seq_cols(SSD_STATE, 2 * SSD_INNER // SSD_STATE),
        seq_cols(SSD_STATE, 2 * SSD_INNER // SSD_STATE + n_blk),
        par(SSD_CONV_W, SSD_GW, 0), par(SSD_CONV_W, SSD_STATE, SSD_INNER // SSD_STATE),
        par(SSD_CONV_W, SSD_STATE, SSD_INNER // SSD_STATE + n_blk),
        par(1, SSD_GW, 0), par(1, SSD_STATE, SSD_INNER // SSD_STATE), par(1, SSD_STATE, SSD_INNER // SSD_STATE + n_blk),
        pl.BlockSpec((2, None, length, SSD_R), lambda b, g: (0, g, b0 + b, 0)),
        pl.BlockSpec((2, None, SSD_R, length), lambda b, g: (0, g, 0, b0 + b)),
        pl.BlockSpec((2, None, 1, SSD_R), lambda b, g: (0, g, 0, 0)),
        pl.BlockSpec((2, None, SSD_R, 1), lambda b, g: (0, g, 0, 0)),
        pl.BlockSpec((2, None, 1, SSD_R), lambda b, g: (0, g, 0, 0)),
        pl.BlockSpec((2, None, SSD_R, 1), lambda b, g: (0, g, 0, 0)),
        par(1, SSD_GW, 0), par(1, SSD_GW, 0),
        pl.BlockSpec((SSD_R, SSD_GW), lambda b, g: (0, 0)),
    ]
    expand = jnp.asarray(np.kron(np.eye(SSD_R), np.ones((1, SSD_HEADDIM))), F32)
    args = [proj, proj, proj, proj, conv_w, conv_w, conv_w, conv_b, conv_b, conv_b,
            dt_col, dt_row, dtb_col, dtb_row, al_col, al_row, d_skip_ch, norm_w, expand]
    if s0 is not None:
        in_specs.append(pl.BlockSpec((None, None, 2, SSD_STATE, SSD_GW), lambda b, g: (b, g, 0, 0, 0)))
        args.append(s0)
    out_specs = [pl.BlockSpec((length, SSD_GW), lambda b, g: (b, g))]
    out_shape = [jax.ShapeDtypeStruct((n_seq * length, SSD_INNER), BF16)]
    if emit_state:
        out_specs.append(pl.BlockSpec((None, None, 2, SSD_STATE, SSD_GW), lambda b, g: (b, g, 0, 0, 0)))
        out_shape.append(jax.ShapeDtypeStruct((n_seq, SSD_GROUPS, 2, SSD_STATE, SSD_GW), F32))
    return pl.pallas_call(
        functools.partial(_ssd_kernel, length=length, has_init=s0 is not None, emit_state=emit_state),
        grid=(n_seq, SSD_GROUPS),
        in_specs=in_specs,
        out_specs=out_specs,
        out_shape=out_shape,
        scratch_shapes=[
            pltpu.VMEM((length, SSD_GW), F32), pltpu.VMEM((length, SSD_STATE), F32),
            pltpu.VMEM((length, SSD_STATE), F32), pltpu.VMEM((length, SSD_GW), F32),
            pltpu.VMEM((SSD_STATE, SSD_GW), F32),
        ],
        compiler_params=_cparams(("arbitrary", "arbitrary")),
        name="ssd_scan",
    )(*args)


def _ssd_state_to_kernel(s):
    b = s.shape[0]
    s = s.reshape(b, 2, SSD_GROUPS, SSD_R, SSD_HEADDIM, SSD_STATE)
    return s.transpose(0, 2, 1, 5, 3, 4).reshape(b, SSD_GROUPS, 2, SSD_STATE, SSD_GW)


def _ssd_state_from_kernel(s):
    b = s.shape[0]
    s = s.reshape(b, SSD_GROUPS, 2, SSD_STATE, SSD_R, SSD_HEADDIM)
    return s.transpose(0, 2, 1, 4, 5, 3).reshape(b, 2, SSD_HEADS, SSD_HEADDIM, SSD_STATE)


def ssd_mixer(h, wl, w_in, conv_w, conv_b, dt_bias, a_log, d_skip, norm_w, state_in):
    proj = matmul(h, w_in, wl, tn=1152)
    t = proj.shape[0]
    dt_raw = proj[:, SSD_INNER + SSD_CONV_DIM:].reshape(t, 2, SSD_GROUPS, SSD_R)
    dt_col = dt_raw.transpose(1, 2, 0, 3)
    dt_row = dt_raw.transpose(1, 2, 3, 0)
    per_group = lambda p: p[wl].reshape(2, SSD_GROUPS, SSD_R)
    dtb, al = per_group(dt_bias), per_group(a_log)
    small = (dtb[:, :, None, :], dtb[:, :, :, None], al[:, :, None, :], al[:, :, :, None])
    n_ssd = conv_b.shape[0]
    common = (wl, conv_w, conv_b.reshape(n_ssd, 1, SSD_CONV_DIM)) + small + (
        jnp.repeat(d_skip, SSD_HEADDIM, axis=-1).reshape(n_ssd, 1, SSD_INNER), norm_w.reshape(n_ssd, 1, SSD_INNER))
    y_p, st = ssd_scan(proj, 0, BATCH, SEQ, dt_col, dt_row, *common, None, True)
    (y_s,) = ssd_scan(proj, T_PROMPT, DEC_BATCH, DEC_SEQ, dt_col, dt_row, *common,
                      _ssd_state_to_kernel(state_in), False)
    return jnp.concatenate([y_p, y_s], axis=0), _ssd_state_from_kernel(st)


HG_HALF = HG_CHUNK // 2


def _block_cumsum(x, row, reverse):
    n = x.shape[0]
    within = row % HG_CHUNK
    step = 1
    while step < HG_CHUNK:
        if reverse:
            shifted = pltpu.roll(x, n - step, 0)
            ok = within < HG_CHUNK - step
        else:
            shifted = pltpu.roll(x, step, 0)
            ok = within >= step
        x = x + jnp.where(ok, shifted, 0.0)
        step *= 2
    return x


def _hg_kernel(*refs, length, has_init, emit_state):
    q_ref, ff_ref, fb_ref, v_ref, gate_ref, lb_ref, gn_ref = refs[:7]
    pos = 7
    if has_init:
        s0_ref = refs[pos]
        pos += 1
    o_ref = refs[pos]
    pos += 1
    if emit_state:
        so_ref = refs[pos]
        pos += 1
    qs_s, g_s, k_s, o_s, st_s = refs[pos:pos + 5]

    row = lax.broadcasted_iota(jnp.int32, (length, HG_DK), 0)
    sub = lax.broadcasted_iota(jnp.int32, (HG_HALF, HG_DK), 0)
    qs_s[...] = _silu(q_ref[...])
    n_blocks = length // HG_CHUNK

    for d, f_ref in enumerate((ff_ref, fb_ref)):
        lb = lb_ref[d]
        f = f_ref[...]
        log_sig = jnp.minimum(f, 0.0) - jnp.log1p(jnp.exp(-jnp.abs(f)))
        a = jnp.log(lb)
        b = jnp.log1p(-lb) + log_sig
        log_g = jnp.maximum(a, b) + jnp.log1p(jnp.exp(-jnp.abs(a - b)))
        g_s[...] = _block_cumsum(log_g, row, reverse=(d == 1))
        k_s[...] = (1.0 - lb) * jax.nn.sigmoid(-f)
        if has_init:
            st_s[...] = s0_ref[d].T
        else:
            st_s[...] = jnp.zeros_like(st_s)

        def block(ci, carry, d=d):
            c = ci if d == 0 else n_blocks - 1 - ci
            rows = pl.ds(pl.multiple_of(c * HG_CHUNK, HG_CHUNK), HG_CHUNK)
            gc = g_s[rows, :]
            q = qs_s[rows, :]
            k = k_s[rows, :]
            v = v_ref[rows, :]
            st = st_s[...]
            g_end = gc[HG_CHUNK - 1:HG_CHUNK, :] if d == 0 else gc[0:1, :]
            o = _nt_dot((q * jnp.exp(gc)).astype(BF16), st.astype(BF16))
            halves = []
            for hh in range(2):
                gh = gc[hh * HG_HALF:(hh + 1) * HG_HALF]
                qh = q[hh * HG_HALF:(hh + 1) * HG_HALF]
                acc = o[hh * HG_HALF:(hh + 1) * HG_HALF]
                for s in range(HG_CHUNK):
                    lo, hi = hh * HG_HALF, (hh + 1) * HG_HALF - 1
                    if (d == 0 and s > hi) or (d == 1 and s < lo):
                        continue
                    w = qh * k[s:s + 1] * jnp.exp(jnp.minimum(gh - gc[s:s + 1], 0.0))
                    if d == 0 and s > lo:
                        w = jnp.where(sub >= s - lo, w, 0.0)
                    if d == 1 and s < hi:
                        w = jnp.where(sub <= s - lo, w, 0.0)
                    acc = acc + jnp.sum(w, axis=-1, keepdims=True) * v[s:s + 1]
                halves.append(acc)
            o_full = jnp.concatenate(halves, axis=0)
            if d == 0:
                o_s[rows, :] = o_full
            else:
                o_s[rows, :] += o_full
            k_hat = k * jnp.exp(g_end - gc)
            upd = lax.dot_general(v.astype(BF16), k_hat.astype(BF16), (((0,), (0,)), ((), ())),
                                  preferred_element_type=F32)
            st_s[...] = st * jnp.exp(g_end) + upd
            return carry

        lax.fori_loop(0, n_blocks, block, 0)
        if emit_state:
            so_ref[d] = st_s[...].T

    o = o_s[...]
    o = o * lax.rsqrt(jnp.mean(o * o, axis=-1, keepdims=True) + EPS) * gn_ref[...]
    o_ref[...] = (o * _silu(gate_ref[...])).astype(o_ref.dtype)


def hgrn2_scan(proj, row0, n_seq, length, lb, g_norm, wl, s0, emit_state):
    b0 = row0 // length
    col = lambda base: pl.BlockSpec((length, HG_DK), lambda b, h: (b0 + b, base * HG_HEADS + h))
    in_specs = [col(0), col(1), col(2), col(3), col(4),
                pl.BlockSpec((2, 1, HG_DK), lambda b, h: (0, 0, h)),
                pl.BlockSpec((None, 1, HG_DV), lambda b, h: (wl, 0, 0))]
    args = [proj, proj, proj, proj, proj, lb, g_norm]
    state_spec = pl.BlockSpec((None, 2, None, HG_DK, HG_DV), lambda b, h: (b, 0, h, 0, 0))
    if s0 is not None:
        in_specs.append(state_spec)
        args.append(s0)
    out_specs = [pl.BlockSpec((length, HG_DV), lambda b, h: (b, h))]
    out_shape = [jax.ShapeDtypeStruct((n_seq * length, D_MODEL), BF16)]
    if emit_state:
        out_specs.append(state_spec)
        out_shape.append(jax.ShapeDtypeStruct((n_seq, 2, HG_HEADS, HG_DK, HG_DV), F32))
    return pl.pallas_call(
        functools.partial(_hg_kernel, length=length, has_init=s0 is not None, emit_state=emit_state),
        grid=(n_seq, HG_HEADS),
        in_specs=in_specs,
        out_specs=out_specs,
        out_shape=out_shape,
        scratch_shapes=[pltpu.VMEM((length, HG_DK), F32) for _ in range(4)] + [pltpu.VMEM((HG_DV, HG_DK), F32)],
        compiler_params=_cparams(("arbitrary", "arbitrary")),
        name="hgrn2_scan",
    )(*args)


def hgrn2_mixer(h, wl, w_in, lb, g_norm, state_in):
    proj = matmul(h, w_in, wl)
    lb3 = lb.reshape(2, 1, HG_FDIM)
    gn3 = g_norm.reshape(-1, 1, HG_DV)
    o_p, st = hgrn2_scan(proj, 0, BATCH, SEQ, lb3, gn3, wl, None, True)
    (o_s,) = hgrn2_scan(proj, T_PROMPT, DEC_BATCH, DEC_SEQ, lb3, gn3, wl, state_in, False)
    return jnp.concatenate([o_p, o_s], axis=0), st


def _final_norm_kernel(x_ref, g_ref, o_ref):
    x = x_ref[...]
    o_ref[...] = x * lax.rsqrt(jnp.mean(x * x, axis=-1, keepdims=True) + EPS) * g_ref[...]


def final_norm(x, gain):
    t = x.shape[0]
    return pl.pallas_call(
        _final_norm_kernel,
        grid=(t // NORM_TM,),
        in_specs=[pl.BlockSpec((NORM_TM, D_MODEL), lambda i: (i, 0)), pl.BlockSpec((1, D_MODEL), lambda i: (0, 0))],
        out_specs=pl.BlockSpec((NORM_TM, D_MODEL), lambda i: (i, 0)),
        out_shape=jax.ShapeDtypeStruct((t, D_MODEL), F32),
        compiler_params=_cparams(("arbitrary",)),
        name="final_norm",
    )(x, gain.reshape(1, D_MODEL))


def kernel(x_prompt, x_sample, cache_attn_k, cache_attn_v, state_hgrn, state_ssd, c, c_ctx, ada_w, ada_b, norm_mix, norm_ffn, norm_final, hy_w_in, hy_w_short, hy_filt_w1, hy_filt_b1, hy_filt_w_hid, hy_filt_b_hid, hy_filt_freq, hy_filt_w_out, hy_skip, hy_w_out, hg_w_in, hg_lb, hg_norm, hg_w_o, ssd_w_in, ssd_conv_w, ssd_conv_b, ssd_dt_bias, ssd_a_log, ssd_d, ssd_norm, ssd_w_out, at_w_qkv, at_sink, at_w_o, moe_w_router, moe_b_router, moe_w_in, moe_b_in, moe_w_out, moe_b_out):
    x = jnp.concatenate([x_prompt.reshape(T_PROMPT, D_MODEL), x_sample.reshape(T_SAMPLE, D_MODEL)], axis=0)
    cvec = jnp.zeros((N_CVEC, D_MODEL), F32).at[0].set(c_ctx).at[1:1 + DEC_BATCH].set(c)
    mods = ada_mods(cvec, ada_w, ada_b)
    lbs = jax.nn.softmax(hg_lb.astype(F32), axis=1)
    lbs = jnp.cumsum(lbs, axis=1) - lbs[:, :1]
    w_router_t = moe_w_router.transpose(0, 2, 1)
    new_k, new_v, new_hg, new_ssd = [], [], [], []
    for li in range(DEPTH):
        kind, j = li % N_MIXERS, li // N_MIXERS
        h = norm_modulate(x, norm_mix, mods, li, 0)
        if kind == 0:
            y = hyena_mixer(h, j, hy_w_in, hy_w_short, hy_filt_w1, hy_filt_b1, hy_filt_w_hid, hy_filt_b_hid,
                            hy_filt_freq, hy_filt_w_out, hy_skip)
            w_o = hy_w_out
        elif kind == 1:
            y, st = hgrn2_mixer(h, j, hg_w_in, lbs[:, li], hg_norm, state_hgrn[:, j])
            new_hg.append(st)
            w_o = hg_w_o
        elif kind == 2:
            y, st = ssd_mixer(h, j, ssd_w_in, ssd_conv_w, ssd_conv_b, ssd_dt_bias, ssd_a_log, ssd_d, ssd_norm,
                              state_ssd[:, j])
            new_ssd.append(st)
            w_o = ssd_w_out
        else:
            y, kc, vc = attention_mixer(h, at_w_qkv, j, at_sink[j], cache_attn_k[:, j], cache_attn_v[:, j])
            new_k.append(kc)
            new_v.append(vc)
            w_o = at_w_o
        x = matmul_residual(y, w_o, j, x, mods, li, 2)
        x = moe_layer(x, li, mods, norm_ffn, w_router_t, moe_b_router, moe_w_in, moe_b_in, moe_w_out, moe_b_out)
    y = final_norm(x, norm_final)
    y_prompt = y[:T_PROMPT].reshape(BATCH, SEQ, D_MODEL)
    y_sample = y[T_PROMPT:].reshape(DEC_BATCH, DEC_SEQ, D_MODEL)
    return (y_prompt, y_sample, jnp.stack(new_k, axis=1), jnp.stack(new_v, axis=1),
            jnp.stack(new_hg, axis=1), jnp.stack(new_ssd, axis=1))
```

```python
import functools
import math

import numpy as np
import jax
import jax.numpy as jnp
from jax import lax
from jax.experimental import pallas as pl
from jax.experimental.pallas import tpu as pltpu

F32 = jnp.float32
BF16 = jnp.bfloat16
HIGHEST = lax.Precision.HIGHEST

D_MODEL = 2048
BATCH = 32
SEQ = 256
DEPTH = 4
DEC_BATCH = 2
DEC_SEQ = 1024
PAST_LEN = 256
GRID_W = 64
N_MIXERS = 4
N_MOD = 6
EPS = 1e-6

HY_ORDER = 2
HY_SHORT_W = 3
HY_EMB_DIM = 33
HY_BANDS = (HY_EMB_DIM - 1) // 2
HY_FILTER_W = 64
HY_N_SIN = 3
HY_DECAY_TARGET = 1e-2
HY_FAST_DECAY = 0.3
HY_SLOW_DECAY = 1.5

HG_HEADS = D_MODEL // 128
HG_DK = 128
HG_DV = D_MODEL // HG_HEADS
HG_FDIM = HG_HEADS * HG_DK
HG_IN_DIM = 3 * HG_FDIM + 2 * D_MODEL
HG_CHUNK = 16

SSD_INNER = 2 * D_MODEL
SSD_HEADDIM = 64
SSD_HEADS = SSD_INNER // SSD_HEADDIM
SSD_GROUPS = 8
SSD_STATE = 128
SSD_CONV_W = 5
SSD_GN = SSD_GROUPS * SSD_STATE
SSD_CONV_DIM = SSD_INNER + 2 * SSD_GN
SSD_IN_DIM = SSD_INNER + SSD_CONV_DIM + 2 * SSD_HEADS

AT_HEAD_DIM = 64
AT_HEADS = D_MODEL // AT_HEAD_DIM
AT_KV_HEADS = AT_HEADS // 8
AT_REP = AT_HEADS // AT_KV_HEADS
AT_WINDOW = 128
AT_BLOCK = 128
ROPE_THETA = 10000.0

N_EXPERTS = 32
TOP_K = 4
D_EXPERT = D_MODEL
SWIGLU_ALPHA = 1.702
SWIGLU_LIMIT = 7.0

T_PROMPT = BATCH * SEQ
T_SAMPLE = DEC_BATCH * DEC_SEQ
T_ALL = T_PROMPT + T_SAMPLE
N_CVEC = 8

VMEM_LIMIT = 56 * 1024 * 1024


def _cparams(sem):
    return pltpu.CompilerParams(dimension_semantics=sem, vmem_limit_bytes=VMEM_LIMIT)


def _mod_row(tile_idx, tile_rows):
    start = tile_idx * tile_rows
    return jnp.where(start < T_PROMPT, 0, 1 + (start - T_PROMPT) // DEC_SEQ)


ADA_TN = 1024


def _ada_kernel(c_ref, w_ref, b_ref, o_ref):
    c = c_ref[...]
    s = c * jax.nn.sigmoid(c)
    o_ref[...] = jnp.dot(s.astype(BF16), w_ref[...].astype(BF16), preferred_element_type=F32) + b_ref[...]


def ada_mods(cvec, ada_w, ada_b):
    n = N_MOD * D_MODEL
    out = pl.pallas_call(
        _ada_kernel,
        grid=(DEPTH, n // ADA_TN),
        in_specs=[
            pl.BlockSpec((N_CVEC, D_MODEL), lambda l, j: (0, 0)),
            pl.BlockSpec((None, D_MODEL, ADA_TN), lambda l, j: (l, 0, j)),
            pl.BlockSpec((None, 1, ADA_TN), lambda l, j: (l, 0, j)),
        ],
        out_specs=pl.BlockSpec((None, N_CVEC, ADA_TN), lambda l, j: (l, 0, j)),
        out_shape=jax.ShapeDtypeStruct((DEPTH, N_CVEC, n), F32),
        compiler_params=_cparams(("arbitrary", "arbitrary")),
        name="ada_mods",
    )(cvec, ada_w, ada_b.reshape(DEPTH, 1, n))
    return out.reshape(DEPTH, N_CVEC * N_MOD, 1, D_MODEL)


def _mod_spec(li, comp, tile_rows):
    return pl.BlockSpec((None, None, 1, D_MODEL),
                        lambda i, *_: (li, _mod_row(i, tile_rows) * N_MOD + comp, 0, 0))


NORM_TM = 256


def _norm_mod(x, g, sh, sc):
    y = x * lax.rsqrt(jnp.mean(x * x, axis=-1, keepdims=True) + EPS) * g
    return y * (1.0 + sc) + sh


def _norm_kernel(x_ref, g_ref, sh_ref, sc_ref, o_ref):
    o_ref[...] = _norm_mod(x_ref[...], g_ref[...], sh_ref[...], sc_ref[...]).astype(o_ref.dtype)


def norm_modulate(x, gains, mods, li, shift_comp):
    t = x.shape[0]
    return pl.pallas_call(
        _norm_kernel,
        grid=(t // NORM_TM,),
        in_specs=[
            pl.BlockSpec((NORM_TM, D_MODEL), lambda i: (i, 0)),
            pl.BlockSpec((None, 1, D_MODEL), lambda i: (li, 0, 0)),
            _mod_spec(li, shift_comp, NORM_TM),
            _mod_spec(li, shift_comp + 1, NORM_TM),
        ],
        out_specs=pl.BlockSpec((NORM_TM, D_MODEL), lambda i: (i, 0)),
        out_shape=jax.ShapeDtypeStruct((t, D_MODEL), BF16),
        compiler_params=_cparams(("arbitrary",)),
        name="norm_modulate",
    )(x, gains.reshape(DEPTH, 1, D_MODEL), mods, mods)


def _norm_router_kernel(x_ref, g_ref, sh_ref, sc_ref, wr_ref, br_ref, o_ref, ti_ref, tg_ref):
    h = _norm_mod(x_ref[...], g_ref[...], sh_ref[...], sc_ref[...])
    o_ref[...] = _pack_bf16_pairs(h)
    logits = lax.dot_general(wr_ref[...], h, (((1,), (1,)), ((), ())),
                             precision=HIGHEST, preferred_element_type=F32) + br_ref[...]
    eidx = lax.broadcasted_iota(jnp.int32, logits.shape, 0)
    vals = logits
    top_v = []
    for k in range(TOP_K):
        m = jnp.max(vals, axis=0, keepdims=True)
        sel = jnp.min(jnp.where(vals == m, eidx, N_EXPERTS), axis=0, keepdims=True)
        top_v.append(m)
        ti_ref[k:k + 1, :] = sel
        vals = jnp.where(eidx == sel, -jnp.inf, vals)
    ex = [jnp.exp(v - top_v[0]) for v in top_v]
    den = ex[0] + ex[1] + ex[2] + ex[3]
    for k in range(TOP_K):
        tg_ref[k:k + 1, :] = ex[k] / den


def norm_router(x, gains, mods, li, w_router_t, b_router):
    t = x.shape[0]
    return pl.pallas_call(
        _norm_router_kernel,
        grid=(t // NORM_TM,),
        in_specs=[
            pl.BlockSpec((NORM_TM, D_MODEL), lambda i: (i, 0)),
            pl.BlockSpec((None, 1, D_MODEL), lambda i: (li, 0, 0)),
            _mod_spec(li, 3, NORM_TM),
            _mod_spec(li, 4, NORM_TM),
            pl.BlockSpec((None, N_EXPERTS, D_MODEL), lambda i: (li, 0, 0)),
            pl.BlockSpec((None, N_EXPERTS, 1), lambda i: (li, 0, 0)),
        ],
        out_specs=[
            pl.BlockSpec((NORM_TM, D_MODEL // 2), lambda i: (i, 0)),
            pl.BlockSpec((TOP_K, NORM_TM), lambda i: (0, i)),
            pl.BlockSpec((TOP_K, NORM_TM), lambda i: (0, i)),
        ],
        out_shape=[
            jax.ShapeDtypeStruct((t, D_MODEL // 2), jnp.uint32),
            jax.ShapeDtypeStruct((TOP_K, t), jnp.int32),
            jax.ShapeDtypeStruct((TOP_K, t), F32),
        ],
        compiler_params=_cparams(("arbitrary",)),
        name="norm_router",
    )(x, gains.reshape(DEPTH, 1, D_MODEL), mods, mods, w_router_t, b_router.reshape(DEPTH, N_EXPERTS, 1))


MM_TM = 1024


def _mm_kernel(x_ref, w_ref, o_ref, wb_ref):
    @pl.when(pl.program_id(1) == 0)
    def _():
        wb_ref[...] = w_ref[...].astype(BF16)

    o_ref[...] = jnp.dot(x_ref[...], wb_ref[...], preferred_element_type=F32).astype(o_ref.dtype)


def _mm_res_kernel(x_ref, w_ref, r_ref, g_ref, o_ref, wb_ref):
    @pl.when(pl.program_id(1) == 0)
    def _():
        wb_ref[...] = w_ref[...].astype(BF16)

    o_ref[...] = r_ref[...] + g_ref[...] * jnp.dot(x_ref[...], wb_ref[...], preferred_element_type=F32)


def _w_spec(w, wl, k, tn):
    if w.ndim == 3:
        return pl.BlockSpec((None, k, tn), lambda j, i: (wl, 0, j))
    return pl.BlockSpec((k, tn), lambda j, i: (0, j))


def matmul(x, w, wl=0, out_dtype=F32, tn=1024):
    t, k = x.shape
    n = w.shape[-1]
    assert n % tn == 0 and t % MM_TM == 0
    return pl.pallas_call(
        _mm_kernel,
        grid=(n // tn, t // MM_TM),
        in_specs=[pl.BlockSpec((MM_TM, k), lambda j, i: (i, 0)), _w_spec(w, wl, k, tn)],
        out_specs=pl.BlockSpec((MM_TM, tn), lambda j, i: (i, j)),
        out_shape=jax.ShapeDtypeStruct((t, n), out_dtype),
        scratch_shapes=[pltpu.VMEM((k, tn), BF16)],
        compiler_params=_cparams(("arbitrary", "arbitrary")),
        name="matmul",
    )(x, w)


def matmul_residual(x, w, wl, res, mods, li, gate_comp, tn=512):
    t, k = x.shape
    n = w.shape[-1]
    assert n % tn == 0 and t % MM_TM == 0
    gate_spec = pl.BlockSpec((None, None, 1, tn),
                             lambda j, i: (li, _mod_row(i, MM_TM) * N_MOD + gate_comp, 0, j))
    return pl.pallas_call(
        _mm_res_kernel,
        grid=(n // tn, t // MM_TM),
        in_specs=[pl.BlockSpec((MM_TM, k), lambda j, i: (i, 0)), _w_spec(w, wl, k, tn),
                  pl.BlockSpec((MM_TM, tn), lambda j, i: (i, j)), gate_spec],
        out_specs=pl.BlockSpec((MM_TM, tn), lambda j, i: (i, j)),
        out_shape=jax.ShapeDtypeStruct((t, n), F32),
        scratch_shapes=[pltpu.VMEM((k, tn), BF16)],
        compiler_params=_cparams(("arbitrary", "arbitrary")),
        name="matmul_residual",
    )(x, w, res, mods)


MOE_BLK = 256
MOE_SB_BLKS = 8
MOE_RB = MOE_BLK * MOE_SB_BLKS
MOE_TH = 256
MOE_NJ = D_EXPERT // MOE_TH
D_PACK = D_MODEL // 2


def _moe_nblk(t):
    return pl.cdiv(t * TOP_K, MOE_BLK) + N_EXPERTS


def _moe_nsb(t):
    return pl.cdiv(t * TOP_K, MOE_RB) + N_EXPERTS


def moe_dispatch(top_i):
    t = top_i.shape[1]
    moe_rows, moe_nsb = _moe_nblk(t) * MOE_BLK, _moe_nsb(t)
    flat_e = top_i.T.reshape(-1)
    onehot = (flat_e[:, None] == jnp.arange(N_EXPERTS, dtype=jnp.int32)[None, :]).astype(jnp.int32)
    csum = jnp.cumsum(onehot, axis=0)
    rank = jnp.take_along_axis(csum, flat_e[:, None], axis=1)[:, 0] - 1
    counts = csum[-1]
    nblk = (counts + MOE_BLK - 1) // MOE_BLK
    blk_end = jnp.cumsum(nblk)
    blk_start = blk_end - nblk
    pos = blk_start[flat_e] * MOE_BLK + rank
    tok_buf = jnp.zeros((moe_rows,), jnp.int32).at[pos].set(jnp.arange(t * TOP_K, dtype=jnp.int32) // TOP_K)
    n_blk = blk_end[-1]
    nsb_e = (nblk + MOE_SB_BLKS - 1) // MOE_SB_BLKS
    sb_end = jnp.cumsum(nsb_e)
    sb_start = sb_end - nsb_e
    n_sb = sb_end[-1]
    sb = jnp.arange(moe_nsb, dtype=jnp.int32)
    e_of = jnp.minimum(jnp.searchsorted(sb_end, sb, side='right'), N_EXPERTS - 1).astype(jnp.int32)
    local = sb - sb_start[e_of]
    valid = sb < n_sb
    e_last = e_of[jnp.maximum(n_sb - 1, 0)]
    sb_e = jnp.where(valid, e_of, e_last).astype(jnp.int32)
    sb_blk0 = jnp.where(valid, blk_start[e_of] + local * MOE_SB_BLKS, 0).astype(jnp.int32)
    sb_nb = jnp.where(valid, jnp.clip(nblk[e_of] - local * MOE_SB_BLKS, 0, MOE_SB_BLKS), 0).astype(jnp.int32)
    sb_rows = jnp.minimum(sb_blk0[:, None] * MOE_BLK + jnp.arange(MOE_RB, dtype=jnp.int32)[None, :], moe_rows - 1)
    return dict(pos=pos.astype(jnp.int32), tok_sb=tok_buf[sb_rows].reshape(moe_nsb, 1, MOE_RB),
                sb_e=sb_e, sb_blk0=sb_blk0, sb_nb=sb_nb, n_sb=n_sb.reshape(1).astype(jnp.int32))


def _pack_bf16_pairs(h):
    half = h.shape[-1] // 2
    lo = lax.bitcast_convert_type(h[:, :half].astype(BF16).astype(F32), jnp.uint32)
    hi = lax.bitcast_convert_type(h[:, half:].astype(BF16).astype(F32), jnp.uint32)
    return (hi & jnp.uint32(0xFFFF0000)) | (lo >> 16)


def _unpack_bf16_pairs(p):
    lo = lax.bitcast_convert_type(p << 16, F32).astype(BF16)
    hi = lax.bitcast_convert_type(p & jnp.uint32(0xFFFF0000), F32).astype(BF16)
    return lo, hi


MOE_DMA_UNROLL = 8


def _moe_kernel(sbe_ref, sbb_ref, sbn_ref, nsb_ref,
                tok_ref, hp_ref, wg_ref, wu_ref, bg_ref, bu_ref, wo_ref, bo_ref,
                ys_ref,
                xp_buf, xb_buf, y_acc, wg_b, wu_b, wo_b, sem_in, sem_out):
    sb = pl.program_id(0)
    j = pl.program_id(1)
    nb = sbn_ref[sb]
    row0 = sbb_ref[sb] * MOE_BLK

    def rows(rb):
        return pl.ds(pl.multiple_of(rb * MOE_BLK, MOE_BLK), MOE_BLK)

    def row_in(r):
        return pltpu.make_async_copy(hp_ref.at[pl.ds(tok_ref[0, r], 1)], xp_buf.at[pl.ds(r, 1)], sem_in)

    def out_copy(rb):
        return pltpu.make_async_copy(y_acc.at[rows(rb)], ys_ref.at[pl.ds(row0 + rb * MOE_BLK, MOE_BLK)], sem_out)

    def for_blocks(fn):
        def body(rb, c):
            fn(rb)
            return c
        lax.fori_loop(0, nb, body, 0)

    def for_block_rows(fn):
        def per_block(rb):
            def body(i, c):
                fn(rb * MOE_BLK + i)
                return c
            lax.fori_loop(0, MOE_BLK, body, 0, unroll=MOE_DMA_UNROLL)
        for_blocks(per_block)

    @pl.when(jnp.logical_and(nb > 0, j == 0))
    def _():
        for_block_rows(lambda r: row_in(r).start())
        for_block_rows(lambda r: row_in(r).wait())

        def unpack(rb):
            lo, hi = _unpack_bf16_pairs(xp_buf[rows(rb), :])
            xb_buf[rows(rb), :D_PACK] = lo
            xb_buf[rows(rb), D_PACK:] = hi

        for_blocks(unpack)

    def ffn_block(rb, first, last):
        x = xb_buf[rows(rb), :]
        hg = jnp.dot(x, wg_b[...], preferred_element_type=F32) + bg_ref[...]
        hu = jnp.dot(x, wu_b[...], preferred_element_type=F32) + bu_ref[...]
        g = jnp.minimum(hg, SWIGLU_LIMIT)
        u = jnp.clip(hu, -SWIGLU_LIMIT, SWIGLU_LIMIT)
        a = (u + 1.0) * g * jax.nn.sigmoid(SWIGLU_ALPHA * g)
        y = jnp.dot(a.astype(BF16), wo_b[...], preferred_element_type=F32)
        if first:
            y_acc[rows(rb), :] = y + bo_ref[...]
        else:
            y_acc[rows(rb), :] += y
        if last:
            out_copy(rb).start()

    @pl.when(nb > 0)
    def _():
        wg_b[...] = wg_ref[...].astype(BF16)
        wu_b[...] = wu_ref[...].astype(BF16)
        wo_b[...] = wo_ref[...].astype(BF16)

    @pl.when(jnp.logical_and(nb > 0, j == 0))
    def _():
        for_blocks(lambda rb: ffn_block(rb, True, False))

    @pl.when(jnp.logical_and(nb > 0, jnp.logical_and(j > 0, j < MOE_NJ - 1)))
    def _():
        for_blocks(lambda rb: ffn_block(rb, False, False))

    @pl.when(jnp.logical_and(nb > 0, j == MOE_NJ - 1))
    def _():
        for_blocks(lambda rb: ffn_block(rb, False, True))
        for_blocks(lambda rb: out_copy(rb).wait())


def moe_experts(hp, tables, w_in, b_in, w_out, b_out, li):
    n_rows = _moe_nblk(hp.shape[0]) * MOE_BLK

    def live_j(sb, j, nsb):
        return jnp.where(sb < nsb[0], j, MOE_NJ - 1)

    w_in_spec = lambda half: pl.BlockSpec(
        (None, None, D_MODEL, MOE_TH),
        lambda sb, j, sbe, sbb, sbn, nsb: (li, sbe[sb], 0, half * MOE_NJ + live_j(sb, j, nsb)))
    b_in_spec = lambda half: pl.BlockSpec(
        (None, None, 1, MOE_TH),
        lambda sb, j, sbe, sbb, sbn, nsb: (li, sbe[sb], 0, half * MOE_NJ + live_j(sb, j, nsb)))
    return pl.pallas_call(
        _moe_kernel,
        grid_spec=pltpu.PrefetchScalarGridSpec(
            num_scalar_prefetch=4,
            grid=(tables["sb_e"].shape[0], MOE_NJ),
            in_specs=[
                pl.BlockSpec((None, 1, MOE_RB), lambda sb, j, sbe, sbb, sbn, nsb: (sb, 0, 0),
                             memory_space=pltpu.SMEM),
                pl.BlockSpec(memory_space=pl.ANY),
                w_in_spec(0), w_in_spec(1), b_in_spec(0), b_in_spec(1),
                pl.BlockSpec((None, None, MOE_TH, D_MODEL),
                             lambda sb, j, sbe, sbb, sbn, nsb: (li, sbe[sb], live_j(sb, j, nsb), 0)),
                pl.BlockSpec((None, None, 1, D_MODEL), lambda sb, j, sbe, sbb, sbn, nsb: (li, sbe[sb], 0, 0)),
            ],
            out_specs=pl.BlockSpec(memory_space=pl.ANY),
            scratch_shapes=[
                pltpu.VMEM((MOE_RB, D_PACK), jnp.uint32),
                pltpu.VMEM((MOE_RB, D_MODEL), BF16),
                pltpu.VMEM((MOE_RB, D_MODEL), F32),
                pltpu.VMEM((D_MODEL, MOE_TH), BF16),
                pltpu.VMEM((D_MODEL, MOE_TH), BF16),
                pltpu.VMEM((MOE_TH, D_MODEL), BF16),
                pltpu.SemaphoreType.DMA(()),
                pltpu.SemaphoreType.DMA(()),
            ],
        ),
        out_shape=jax.ShapeDtypeStruct((n_rows, D_MODEL), F32),
        compiler_params=_cparams(("arbitrary", "arbitrary")),
        name="moe_experts",
    )(tables["sb_e"], tables["sb_blk0"], tables["sb_nb"], tables["n_sb"],
      tables["tok_sb"], hp, w_in, w_in, b_in.reshape(-1, N_EXPERTS, 1, 2 * D_EXPERT), b_in.reshape(-1, N_EXPERTS, 1, 2 * D_EXPERT),
      w_out, b_out.reshape(-1, N_EXPERTS, 1, D_MODEL))


CMB_TM = 128


def _combine_kernel(pos_ref, posn_ref, ys_ref, x_ref, tg_ref, mg_ref, o_ref, buf, sems):
    i = pl.program_id(0)
    nt = pl.num_programs(0)
    slot = i % 2

    def for_rows(idx_ref, dst_slot, fn):
        for k in range(TOP_K):
            def body(r, c, k=k):
                fn(pltpu.make_async_copy(ys_ref.at[pl.ds(idx_ref[0, k * CMB_TM + r], 1)],
                                         buf.at[dst_slot, k, pl.ds(r, 1)], sems.at[dst_slot]))
                return c
            lax.fori_loop(0, CMB_TM, body, 0, unroll=MOE_DMA_UNROLL)

    @pl.when(i == 0)
    def _():
        for_rows(pos_ref, 0, lambda cp: cp.start())

    @pl.when(i + 1 < nt)
    def _():
        for_rows(posn_ref, 1 - slot, lambda cp: cp.start())

    for_rows(pos_ref, slot, lambda cp: cp.wait())
    tg = tg_ref[...]
    y = tg[:, 0:1] * buf[slot, 0]
    for k in range(1, TOP_K):
        y = y + tg[:, k:k + 1] * buf[slot, k]
    o_ref[...] = x_ref[...] + mg_ref[...] * y


def moe_combine(ys, pos, gates_t, x, mods, li):
    t = x.shape[0]
    nt = t // CMB_TM
    pos_tiles = pos.reshape(nt, CMB_TM, TOP_K).transpose(0, 2, 1).reshape(nt, 1, TOP_K * CMB_TM)
    pos_spec = lambda shift: pl.BlockSpec((None, 1, TOP_K * CMB_TM),
                                          lambda i: (jnp.minimum(i + shift, nt - 1), 0, 0), memory_space=pltpu.SMEM)
    return pl.pallas_call(
        _combine_kernel,
        grid=(nt,),
        in_specs=[
            pos_spec(0), pos_spec(1),
            pl.BlockSpec(memory_space=pl.ANY),
            pl.BlockSpec((CMB_TM, D_MODEL), lambda i: (i, 0)),
            pl.BlockSpec((CMB_TM, TOP_K), lambda i: (i, 0)),
            _mod_spec(li, 5, CMB_TM),
        ],
        out_specs=pl.BlockSpec((CMB_TM, D_MODEL), lambda i: (i, 0)),
        out_shape=jax.ShapeDtypeStruct((t, D_MODEL), F32),
        scratch_shapes=[pltpu.VMEM((2, TOP_K, CMB_TM, D_MODEL), F32), pltpu.SemaphoreType.DMA((2,))],
        compiler_params=_cparams(("arbitrary",)),
        name="moe_combine",
    )(pos_tiles, pos_tiles, ys, x, gates_t, mods)


def moe_layer(x, li, mods, norm_ffn, w_router_t, b_router, w_in, b_in, w_out, b_out):
    hp, top_i, top_g = norm_router(x, norm_ffn, mods, li, w_router_t, b_router)
    tables = moe_dispatch(top_i)
    ys = moe_experts(hp, tables, w_in, b_in, w_out, b_out, li)
    return moe_combine(ys, tables["pos"], top_g.T, x, mods, li)


AT_NH = AT_HEADS + 2 * AT_KV_HEADS
AT_SCALE = AT_HEAD_DIM ** -0.5


def _rope_tables(length):
    pos = np.arange(length)
    half, quarter = AT_HEAD_DIM // 2, AT_HEAD_DIM // 4
    inv = ROPE_THETA ** (-np.arange(0, half, 2, dtype=np.float64) / half)
    j = np.arange(AT_HEAD_DIM)
    p = np.where(j[None, :] < half, (pos // GRID_W)[:, None], (pos % GRID_W)[:, None]).astype(np.float64)
    ang = p * inv[(j % half) % quarter][None, :]
    first = (j % half) < quarter
    sin_signed = np.where(first[None, :], -np.sin(ang), np.sin(ang))
    partner = np.where(first, j + quarter, j - quarter)
    perm = np.zeros((AT_HEAD_DIM, AT_HEAD_DIM), np.float32)
    perm[partner, j] = 1.0
    return jnp.asarray(np.cos(ang), F32), jnp.asarray(sin_signed, F32), jnp.asarray(perm)


def _rope_kernel(x_ref, cos_ref, sin_ref, perm_ref, o_ref):
    x = x_ref[...]
    xp = jnp.dot(x, perm_ref[...], precision=HIGHEST, preferred_element_type=F32)
    o_ref[...] = x * cos_ref[...] + xp * sin_ref[...]


def rope_heads(xh, n_heads):
    b, _, length, dh = xh.shape
    cos, sin, perm = _rope_tables(length)
    return pl.pallas_call(
        _rope_kernel,
        grid=(b, n_heads),
        in_specs=[
            pl.BlockSpec((None, None, length, dh), lambda i, h: (i, h, 0, 0)),
            pl.BlockSpec((length, dh), lambda i, h: (0, 0)),
            pl.BlockSpec((length, dh), lambda i, h: (0, 0)),
            pl.BlockSpec((dh, dh), lambda i, h: (0, 0)),
        ],
        out_specs=pl.BlockSpec((None, None, length, dh), lambda i, h: (i, h, 0, 0)),
        out_shape=jax.ShapeDtypeStruct((b, n_heads, length, dh), F32),
        compiler_params=_cparams(("arbitrary", "arbitrary")),
        name="rope_heads",
    )(xh, cos, sin, perm)


def _nt_dot(a, b):
    return lax.dot_general(a, b, (((1,), (1,)), ((), ())), preferred_element_type=F32)


def _attn_ctx_kernel(sink_ref, q_ref, k_ref, v_ref, o_ref):
    g = pl.program_id(1)
    k = k_ref[...].astype(BF16)
    v = v_ref[...].astype(BF16)
    for r in range(AT_REP):
        q = (q_ref[r] * AT_SCALE).astype(BF16)
        s = _nt_dot(q, k)
        sk = sink_ref[g * AT_REP + r]
        m = jnp.maximum(jnp.max(s, axis=-1, keepdims=True), sk)
        p = jnp.exp(s - m)
        den = jnp.sum(p, axis=-1, keepdims=True) + jnp.exp(sk - m)
        o = jnp.dot(p.astype(BF16), v, preferred_element_type=F32) / den
        o_ref[r] = o.astype(o_ref.dtype)


def attn_context(qkvh, sink):
    b, _, length, dh = qkvh.shape
    return pl.pallas_call(
        _attn_ctx_kernel,
        grid=(b, AT_KV_HEADS),
        in_specs=[
            pl.BlockSpec(memory_space=pltpu.SMEM),
            pl.BlockSpec((None, AT_REP, length, dh), lambda i, g: (i, g, 0, 0)),
            pl.BlockSpec((None, None, length, dh), lambda i, g: (i, AT_HEADS + g, 0, 0)),
            pl.BlockSpec((None, None, length, dh), lambda i, g: (i, AT_HEADS + AT_KV_HEADS + g, 0, 0)),
        ],
        out_specs=pl.BlockSpec((None, AT_REP, length, dh), lambda i, g: (i, g, 0, 0)),
        out_shape=jax.ShapeDtypeStruct((b, AT_HEADS, length, dh), BF16),
        compiler_params=_cparams(("arbitrary", "arbitrary")),
        name="attn_context",
    )(sink, qkvh, qkvh, qkvh)


def _attn_lat_kernel(sink_ref, q_ref, k0_ref, k1_ref, k2_ref, v0_ref, v1_ref, v2_ref, kc_ref, vc_ref, o_ref, *, length):
    g = pl.program_id(1)
    i = pl.program_id(2)
    kb = jnp.concatenate([k0_ref[...], k1_ref[...], k2_ref[...]], axis=0).astype(BF16)
    vb = jnp.concatenate([v0_ref[...], v1_ref[...], v2_ref[...]], axis=0).astype(BF16)
    kc = kc_ref[...].astype(BF16)
    vc = vc_ref[...].astype(BF16)
    qpos = i * AT_BLOCK + lax.broadcasted_iota(jnp.int32, (AT_BLOCK, 3 * AT_BLOCK), 0)
    kpos = (i - 1) * AT_BLOCK + lax.broadcasted_iota(jnp.int32, (AT_BLOCK, 3 * AT_BLOCK), 1)
    allowed = (jnp.abs(qpos - kpos) <= AT_WINDOW) & (kpos >= 0) & (kpos < length)
    for r in range(AT_REP):
        q = (q_ref[r] * AT_SCALE).astype(BF16)
        s1 = jnp.where(allowed, _nt_dot(q, kb), -jnp.inf)
        s2 = _nt_dot(q, kc)
        sk = sink_ref[g * AT_REP + r]
        m = jnp.maximum(jnp.maximum(jnp.max(s1, axis=-1, keepdims=True), jnp.max(s2, axis=-1, keepdims=True)), sk)
        p1 = jnp.exp(s1 - m)
        p2 = jnp.exp(s2 - m)
        den = jnp.sum(p1, axis=-1, keepdims=True) + jnp.sum(p2, axis=-1, keepdims=True) + jnp.exp(sk - m)
        o = (jnp.dot(p1.astype(BF16), vb, preferred_element_type=F32)
             + jnp.dot(p2.astype(BF16), vc, preferred_element_type=F32)) / den
        o_ref[r] = o.astype(o_ref.dtype)


def attn_latent(qk_rope, qkvh, k_ctx, v_ctx, sink):
    b, _, length, dh = qkvh.shape
    nblk = length // AT_BLOCK
    lc = k_ctx.shape[2]
    band = lambda head0, shift: pl.BlockSpec(
        (None, None, AT_BLOCK, dh), lambda bi, g, i: (bi, head0 + g, jnp.clip(i + shift, 0, nblk - 1), 0))
    ctx = pl.BlockSpec((None, None, lc, dh), lambda bi, g, i: (bi, g, 0, 0))
    return pl.pallas_call(
        functools.partial(_attn_lat_kernel, length=length),
        grid=(b, AT_KV_HEADS, nblk),
        in_specs=[
            pl.BlockSpec(memory_space=pltpu.SMEM),
            pl.BlockSpec((None, AT_REP, AT_BLOCK, dh), lambda bi, g, i: (bi, g, i, 0)),
            band(AT_HEADS, -1), band(AT_HEADS, 0), band(AT_HEADS, 1),
            band(AT_HEADS + AT_KV_HEADS, -1), band(AT_HEADS + AT_KV_HEADS, 0), band(AT_HEADS + AT_KV_HEADS, 1),
            ctx, ctx,
        ],
        out_specs=pl.BlockSpec((None, AT_REP, AT_BLOCK, dh), lambda bi, g, i: (bi, g, i, 0)),
        out_shape=jax.ShapeDtypeStruct((b, AT_HEADS, length, dh), BF16),
        compiler_params=_cparams(("arbitrary", "arbitrary", "arbitrary")),
        name="attn_latent",
    )(sink, qk_rope, qk_rope, qk_rope, qk_rope, qkvh, qkvh, qkvh, k_ctx, v_ctx)


def attention_mixer(h, w_qkv, wl, sink, cache_k, cache_v):
    qkv = matmul(h, w_qkv, wl, tn=512)
    nkv = AT_KV_HEADS * AT_HEAD_DIM
    new_k = qkv[:T_PROMPT, D_MODEL:D_MODEL + nkv].reshape(BATCH, SEQ, AT_KV_HEADS, AT_HEAD_DIM)
    new_v = qkv[:T_PROMPT, D_MODEL + nkv:].reshape(BATCH, SEQ, AT_KV_HEADS, AT_HEAD_DIM)
    qkv_p = qkv[:T_PROMPT].reshape(BATCH, SEQ, AT_NH, AT_HEAD_DIM).transpose(0, 2, 1, 3)
    qkv_s = qkv[T_PROMPT:].reshape(DEC_BATCH, DEC_SEQ, AT_NH, AT_HEAD_DIM).transpose(0, 2, 1, 3)
    o_p = attn_context(qkv_p, sink)
    qk_rope = rope_heads(qkv_s, AT_HEADS + AT_KV_HEADS)
    o_s = attn_latent(qk_rope, qkv_s, cache_k.transpose(0, 2, 1, 3), cache_v.transpose(0, 2, 1, 3), sink)
    o = jnp.concatenate([o_p.transpose(0, 2, 1, 3).reshape(T_PROMPT, D_MODEL),
                         o_s.transpose(0, 2, 1, 3).reshape(T_SAMPLE, D_MODEL)], axis=0)
    return o, new_k, new_v


HY_FEAT_PAD = 128


def _dft_matrices(length):
    n = np.arange(length, dtype=np.float64)
    ang = np.pi * np.outer(n, n) / length
    alt = np.where(n % 2 == 0, 1.0, -1.0)
    fwd_im = -np.sin(ang)
    fwd_im[0, :] = alt
    fwd = np.concatenate([np.cos(ang), fwd_im], axis=0)
    wk = np.where(n == 0, 1.0, 2.0)[None, :] / (2.0 * length)
    inv_im = -wk * np.sin(ang)
    inv_im[:, 0] = alt / (2.0 * length)
    inv = np.concatenate([wk * np.cos(ang), inv_im], axis=1)
    return jnp.asarray(fwd, F32).astype(BF16), jnp.asarray(inv, F32).astype(BF16)


def _hyena_features(length):
    t = np.linspace(0.0, 1.0, length)[:, None]
    om = 2.0 * np.pi * np.arange(length)[:, None] / length
    f = np.linspace(1e-4, HY_BANDS - 1, HY_BANDS)[None, :]
    z = np.concatenate([t, np.cos(f * om), -np.sin(f * om)], axis=-1)
    return jnp.asarray(np.pad(z, ((0, 0), (0, HY_FEAT_PAD - HY_EMB_DIM))), F32)


def _hyena_deltas():
    max_decay = math.log(HY_DECAY_TARGET) / HY_FAST_DECAY
    min_decay = math.log(HY_DECAY_TARGET) / HY_SLOW_DECAY
    return jnp.asarray(np.abs(np.linspace(min_decay, max_decay, D_MODEL)), F32).reshape(1, D_MODEL)


def _hy_filter_kernel(z_ref, w1_ref, b1_ref, wh_ref, bh_ref, fr_ref, wo00, wo01, wo10, wo11, dl_ref, fwd_ref, k_ref,
                      *, length):
    hp = functools.partial(jnp.dot, precision=HIGHEST, preferred_element_type=F32)
    h = jnp.sin(fr_ref[0] * (hp(z_ref[...], w1_ref[...]) + b1_ref[...]))
    for s in range(HY_N_SIN - 1):
        h = jnp.sin(fr_ref[s + 1] * (hp(h, wh_ref[s]) + bh_ref[s]))
    td = dl_ref.shape[-1]
    row = lax.broadcasted_iota(jnp.int32, (length, td), 0)
    t = row.astype(F32) * (1.0 / (length - 1))
    decay = jnp.exp(-t * dl_ref[...])
    fwd = fwd_ref[...]
    for o, (wf, wb) in enumerate(((wo00, wo01), (wo10, wo11))):
        hf = hp(h, wf[...]) * decay
        hb = jnp.where(row == 0, 0.0, hp(h, wb[...]) * decay)
        ks = jnp.dot(fwd, (hf + hb).astype(BF16), preferred_element_type=F32)
        kd = jnp.dot(fwd[length:], (hf - hb).astype(BF16), preferred_element_type=F32)
        k_ref[o, :length, :] = ks[:length]
        k_ref[o, length:, :] = jnp.where(row == 0, ks[length:length + 1], kd)


def hyena_filter_spectrum(length, wl, w1, b1, w_hid, b_hid, freq, w_filt_out, td=512):
    fwd, _ = _dft_matrices(length)
    nj = D_MODEL // td
    w1p = jnp.pad(w1, ((0, 0), (0, HY_FEAT_PAD - HY_EMB_DIM), (0, 0)))
    n_hy = w1.shape[0]
    wout_spec = lambda c: pl.BlockSpec((None, HY_FILTER_W, td), lambda j: (wl, 0, c * nj + j))
    whole = lambda shape: pl.BlockSpec((None,) + shape, lambda j: (wl,) + (0,) * len(shape))
    return pl.pallas_call(
        functools.partial(_hy_filter_kernel, length=length),
        grid=(nj,),
        in_specs=[
            pl.BlockSpec((length, HY_FEAT_PAD), lambda j: (0, 0)),
            whole((HY_FEAT_PAD, HY_FILTER_W)), whole((1, HY_FILTER_W)),
            whole((HY_N_SIN - 1, HY_FILTER_W, HY_FILTER_W)), whole((HY_N_SIN - 1, 1, HY_FILTER_W)),
            whole((HY_N_SIN, 1, HY_FILTER_W)),
            wout_spec(0), wout_spec(1), wout_spec(2), wout_spec(3),
            pl.BlockSpec((1, td), lambda j: (0, j)),
            pl.BlockSpec((2 * length, length), lambda j: (0, 0)),
        ],
        out_specs=pl.BlockSpec((HY_ORDER, 2 * length, td), lambda j: (0, 0, j)),
        out_shape=jax.ShapeDtypeStruct((HY_ORDER, 2 * length, D_MODEL), F32),
        compiler_params=_cparams(("arbitrary",)),
        name="hyena_filter",
    )(_hyena_features(length), w1p, b1.reshape(n_hy, 1, HY_FILTER_W), w_hid,
      b_hid.reshape(n_hy, HY_N_SIN - 1, 1, HY_FILTER_W), freq.reshape(n_hy, HY_N_SIN, 1, HY_FILTER_W),
      w_filt_out, w_filt_out, w_filt_out, w_filt_out, _hyena_deltas(), fwd)


def _hy_conv_kernel(pv_ref, p1_ref, p2_ref, wv_ref, w1_ref, w2_ref, k_ref, skip_ref, fwd_ref, inv_ref, o_ref, *, length):
    td = o_ref.shape[-1]
    row = lax.broadcasted_iota(jnp.int32, (length, td), 0)

    def short_conv(p_ref, w_ref):
        p = p_ref[...]
        w = w_ref[...]
        prev = jnp.where(row == 0, 0.0, pltpu.roll(p, 1, 0))
        nxt = jnp.where(row == length - 1, 0.0, pltpu.roll(p, length - 1, 0))
        return prev * w[0:1] + p * w[1:2] + nxt * w[2:3]

    z = short_conv(pv_ref, wv_ref)
    gates = (short_conv(p1_ref, w1_ref), short_conv(p2_ref, w2_ref))
    fwd = fwd_ref[...]
    inv = inv_ref[...]
    for o in range(HY_ORDER):
        zf = jnp.dot(fwd, z.astype(BF16), preferred_element_type=F32)
        zr, zi = zf[:length], zf[length:]
        kr, ki = k_ref[o, :length, :], k_ref[o, length:, :]
        ii = zi * ki
        pr = zr * kr - jnp.where(row == 0, 0.0, ii)
        pi = jnp.where(row == 0, ii, zr * ki + zi * kr)
        conv = jnp.dot(inv, jnp.concatenate([pr, pi], axis=0).astype(BF16), preferred_element_type=F32)
        z = gates[o] * (conv + z * skip_ref[o:o + 1, :])
    o_ref[...] = z.astype(o_ref.dtype)


def hyena_conv(p, row0, n_seq, length, spectrum, w_short, wl, skip, td):
    fwd, inv = _dft_matrices(length)
    nj = D_MODEL // td
    b0 = row0 // length
    p_spec = lambda c: pl.BlockSpec((length, td), lambda b, j: (b0 + b, c * nj + j))
    w_spec = lambda c: pl.BlockSpec((None, HY_SHORT_W, td), lambda b, j: (wl, 0, c * nj + j))
    return pl.pallas_call(
        functools.partial(_hy_conv_kernel, length=length),
        grid=(n_seq, nj),
        in_specs=[
            p_spec(0), p_spec(1), p_spec(2), w_spec(0), w_spec(1), w_spec(2),
            pl.BlockSpec((HY_ORDER, 2 * length, td), lambda b, j: (0, 0, j)),
            pl.BlockSpec((None, HY_ORDER, td), lambda b, j: (wl, 0, j)),
            pl.BlockSpec((2 * length, length), lambda b, j: (0, 0)),
            pl.BlockSpec((length, 2 * length), lambda b, j: (0, 0)),
        ],
        out_specs=pl.BlockSpec((length, td), lambda b, j: (b, j)),
        out_shape=jax.ShapeDtypeStruct((n_seq * length, D_MODEL), BF16),
        compiler_params=_cparams(("arbitrary", "arbitrary")),
        name="hyena_conv",
    )(p, p, p, w_short, w_short, w_short, spectrum, skip, fwd, inv)


def hyena_mixer(h, wl, w_in, w_short, w1, b1, w_hid, b_hid, freq, w_filt_out, skip):
    p = matmul(h, w_in, wl)
    filt = (w1, b1, w_hid, b_hid, freq, w_filt_out)
    z_p = hyena_conv(p, 0, BATCH, SEQ, hyena_filter_spectrum(SEQ, wl, *filt), w_short, wl, skip, td=D_MODEL)
    z_s = hyena_conv(p, T_PROMPT, DEC_BATCH, DEC_SEQ, hyena_filter_spectrum(DEC_SEQ, wl, *filt), w_short, wl, skip, td=512)
    return jnp.concatenate([z_p, z_s], axis=0)


SSD_Q = 128
SSD_R = SSD_HEADS // SSD_GROUPS
SSD_GW = SSD_R * SSD_HEADDIM


def _shift_rows(x, s, row, length):
    if s == 0:
        return x
    rolled = pltpu.roll(x, (-s) % length, 0)
    ok = (row + s >= 0) & (row + s < length)
    return jnp.where(ok, rolled, 0.0)


def _silu(x):
    return x * jax.nn.sigmoid(x)


def _softplus(x):
    return jnp.maximum(x, 0.0) + jnp.log1p(jnp.exp(-jnp.abs(x)))


def _ssd_kernel(*refs, length, has_init, emit_state):
    (z_ref, x_ref, b_ref, c_ref, wx_ref, wb_ref, wc_ref, bx_ref, bb_ref, bc_ref,
     dtc_ref, dtr_ref, dbc_ref, dbr_ref, alc_ref, alr_ref, dsk_ref, nw_ref, exp_ref) = refs[:19]
    pos = 19
    if has_init:
        s0_ref = refs[pos]
        pos += 1
    o_ref = refs[pos]
    pos += 1
    if emit_state:
        so_ref = refs[pos]
        pos += 1
    xc_s, bc_s, cc_s, y_s, st_s = refs[pos:pos + 5]

    hp = functools.partial(jnp.dot, precision=HIGHEST, preferred_element_type=F32)

    def conv_silu(p_ref, w_ref, bias_ref):
        p = p_ref[...]
        w = w_ref[...]
        row = lax.broadcasted_iota(jnp.int32, p.shape, 0)
        acc = bias_ref[...] + _shift_rows(p, -2, row, length) * w[0:1]
        for k in range(1, SSD_CONV_W):
            acc = acc + _shift_rows(p, k - 2, row, length) * w[k:k + 1]
        return _silu(acc)

    xc_s[...] = conv_silu(x_ref, wx_ref, bx_ref)
    bc_s[...] = conv_silu(b_ref, wb_ref, bb_ref)
    cc_s[...] = conv_silu(c_ref, wc_ref, bc_ref)
    y_s[...] = xc_s[...] * dsk_ref[...]

    expand = exp_ref[...]
    ti = lax.broadcasted_iota(jnp.int32, (SSD_Q, SSD_Q), 0)
    si = lax.broadcasted_iota(jnp.int32, (SSD_Q, SSD_Q), 1)
    lane = lax.broadcasted_iota(jnp.int32, (SSD_Q, 2 * SSD_HEADDIM), 1)
    n_chunks = length // SSD_Q

    for d in range(2):
        causal = (si <= ti) if d == 0 else (si >= ti)
        tri = causal.astype(F32)
        tri_t = ((ti <= si) if d == 0 else (ti >= si)).astype(F32)
        a_col = -jnp.exp(alc_ref[d])
        a_row = -jnp.exp(alr_ref[d])
        if has_init:
            st_s[...] = s0_ref[d]
        else:
            st_s[...] = jnp.zeros_like(st_s)

        def chunk(ci, carry, d=d, causal=causal, tri=tri, tri_t=tri_t, a_col=a_col, a_row=a_row):
            c = ci if d == 0 else n_chunks - 1 - ci
            r0 = pl.multiple_of(c * SSD_Q, SSD_Q)
            rows = pl.ds(r0, SSD_Q)
            dt_c = _softplus(dtc_ref[d, rows, :] + dbc_ref[d])
            dt_r = _softplus(dtr_ref[d, :, rows] + dbr_ref[d])
            acs = hp(tri, dt_c * a_col)
            acs_r = hp(dt_r * a_row, tri_t)
            end = acs[SSD_Q - 1:SSD_Q, :] if d == 0 else acs[0:1, :]
            e_acs = hp(jnp.exp(acs), expand)
            w_in = hp(jnp.exp(end - acs) * dt_c, expand)
            e_end = e_acs[SSD_Q - 1:SSD_Q, :] if d == 0 else e_acs[0:1, :]
            xq = xc_s[rows, :]
            bq = bc_s[rows, :]
            cq = cc_s[rows, :].astype(BF16)
            cb = _nt_dot(cq, bq.astype(BF16))
            st = st_s[...]
            y = e_acs * jnp.dot(cq, st.astype(BF16), preferred_element_type=F32)
            xb = xq.astype(BF16)
            pieces = []
            for pr in range(SSD_R // 2):
                xp = xb[:, pr * 2 * SSD_HEADDIM:(pr + 1) * 2 * SSD_HEADDIM]
                acc = None
                for sub in range(2):
                    r = 2 * pr + sub
                    seg = acs[:, r:r + 1] - acs_r[r:r + 1, :]
                    w = jnp.where(causal, jnp.exp(jnp.where(causal, seg, 0.0)) * cb * dt_r[r:r + 1, :], 0.0)
                    in_head = (lane < SSD_HEADDIM) if sub == 0 else (lane >= SSD_HEADDIM)
                    part = jnp.dot(w.astype(BF16), jnp.where(in_head, xp, jnp.zeros_like(xp)),
                                   preferred_element_type=F32)
                    acc = part if acc is None else acc + part
                pieces.append(acc)
            y_s[rows, :] += y + jnp.concatenate(pieces, axis=1)
            upd = lax.dot_general(bq.astype(BF16), (xq * w_in).astype(BF16), (((0,), (0,)), ((), ())),
                                  preferred_element_type=F32)
            st_s[...] = e_end * st + upd
            return carry

        lax.fori_loop(0, n_chunks, chunk, 0)
        if emit_state:
            so_ref[d] = st_s[...]

    y = y_s[...] * _silu(z_ref[...])
    y = y * lax.rsqrt(jnp.mean(y * y, axis=-1, keepdims=True) + EPS) * nw_ref[...]
    o_ref[...] = y.astype(o_ref.dtype)


def ssd_scan(proj, row0, n_seq, length, dt_col, dt_row, wl, conv_w, conv_b, dtb_col, dtb_row, al_col, al_row,
             d_skip_ch, norm_w, s0, emit_state):
    b0 = row0 // length
    gw_blk = SSD_INNER // SSD_GW
    n_blk = SSD_GN // SSD_STATE
    seq_cols = lambda width, base: pl.BlockSpec((length, width), lambda b, g: (b0 + b, base + g))
    par = lambda rows_, width, base: pl.BlockSpec((None, rows_, width), lambda b, g: (wl, 0, base + g))
    in_specs = [
        seq_cols(SSD_GW, 0),
        seq_cols(SSD_GW, gw_blk),
        seq_cols(SSD_STATE, 2 * SSD_INNER // SSD_STATE),
        seq_cols(SSD_STATE, 2 * SSD_INNER // SSD_STATE + n_blk),
        par(SSD_CONV_W, SSD_GW, 0), par(SSD_CONV_W, SSD_STATE, SSD_INNER // SSD_STATE),
        par(SSD_CONV_W, SSD_STATE, SSD_INNER // SSD_STATE + n_blk),
        par(1, SSD_GW, 0), par(1, SSD_STATE, SSD_INNER // SSD_STATE), par(1, SSD_STATE, SSD_INNER // SSD_STATE + n_blk),
        pl.BlockSpec((2, None, length, SSD_R), lambda b, g: (0, g, b0 + b, 0)),
        pl.BlockSpec((2, None, SSD_R, length), lambda b, g: (0, g, 0, b0 + b)),
        pl.BlockSpec((2, None, 1, SSD_R), lambda b, g: (0, g, 0, 0)),
        pl.BlockSpec((2, None, SSD_R, 1), lambda b, g: (0, g, 0, 0)),
        pl.BlockSpec((2, None, 1, SSD_R), lambda b, g: (0, g, 0, 0)),
        pl.BlockSpec((2, None, SSD_R, 1), lambda b, g: (0, g, 0, 0)),
        par(1, SSD_GW, 0), par(1, SSD_GW, 0),
        pl.BlockSpec((SSD_R, SSD_GW), lambda b, g: (0, 0)),
    ]
    expand = jnp.asarray(np.kron(np.eye(SSD_R), np.ones((1, SSD_HEADDIM))), F32)
    args = [proj, proj, proj, proj, conv_w, conv_w, conv_w, conv_b, conv_b, conv_b,
            dt_col, dt_row, dtb_col, dtb_row, al_col, al_row, d_skip_ch, norm_w, expand]
    if s0 is not None:
        in_specs.append(pl.BlockSpec((None, None, 2, SSD_STATE, SSD_GW), lambda b, g: (b, g, 0, 0, 0)))
        args.append(s0)
    out_specs = [pl.BlockSpec((length, SSD_GW), lambda b, g: (b, g))]
    out_shape = [jax.ShapeDtypeStruct((n_seq * length, SSD_INNER), BF16)]
    if emit_state:
        out_specs.append(pl.BlockSpec((None, None, 2, SSD_STATE, SSD_GW), lambda b, g: (b, g, 0, 0, 0)))
        out_shape.append(jax.ShapeDtypeStruct((n_seq, SSD_GROUPS, 2, SSD_STATE, SSD_GW), F32))
    return pl.pallas_call(
        functools.partial(_ssd_kernel, length=length, has_init=s0 is not None, emit_state=emit_state),
        grid=(n_seq, SSD_GROUPS),
        in_specs=in_specs,
        out_specs=out_specs,
        out_shape=out_shape,
        scratch_shapes=[
            pltpu.VMEM((length, SSD_GW), F32), pltpu.VMEM((length, SSD_STATE), F32),
            pltpu.VMEM((length, SSD_STATE), F32), pltpu.VMEM((length, SSD_GW), F32),
            pltpu.VMEM((SSD_STATE, SSD_GW), F32),
        ],
        compiler_params=_cparams(("arbitrary", "arbitrary")),
        name="ssd_scan",
    )(*args)


def _ssd_state_to_kernel(s):
    b = s.shape[0]
    s = s.reshape(b, 2, SSD_GROUPS, SSD_R, SSD_HEADDIM, SSD_STATE)
    return s.transpose(0, 2, 1, 5, 3, 4).reshape(b, SSD_GROUPS, 2, SSD_STATE, SSD_GW)


def _ssd_state_from_kernel(s):
    b = s.shape[0]
    s = s.reshape(b, SSD_GROUPS, 2, SSD_STATE, SSD_R, SSD_HEADDIM)
    return s.transpose(0, 2, 1, 4, 5, 3).reshape(b, 2, SSD_HEADS, SSD_HEADDIM, SSD_STATE)


def ssd_mixer(h, wl, w_in, conv_w, conv_b, dt_bias, a_log, d_skip, norm_w, state_in):
    proj = matmul(h, w_in, wl, tn=1152)
    t = proj.shape[0]
    dt_raw = proj[:, SSD_INNER + SSD_CONV_DIM:].reshape(t, 2, SSD_GROUPS, SSD_R)
    dt_col = dt_raw.transpose(1, 2, 0, 3)
    dt_row = dt_raw.transpose(1, 2, 3, 0)
    per_group = lambda p: p[wl].reshape(2, SSD_GROUPS, SSD_R)
    dtb, al = per_group(dt_bias), per_group(a_log)
    small = (dtb[:, :, None, :], dtb[:, :, :, None], al[:, :, None, :], al[:, :, :, None])
    n_ssd = conv_b.shape[0]
    common = (wl, conv_w, conv_b.reshape(n_ssd, 1, SSD_CONV_DIM)) + small + (
        jnp.repeat(d_skip, SSD_HEADDIM, axis=-1).reshape(n_ssd, 1, SSD_INNER), norm_w.reshape(n_ssd, 1, SSD_INNER))
    y_p, st = ssd_scan(proj, 0, BATCH, SEQ, dt_col, dt_row, *common, None, True)
    (y_s,) = ssd_scan(proj, T_PROMPT, DEC_BATCH, DEC_SEQ, dt_col, dt_row, *common,
                      _ssd_state_to_kernel(state_in), False)
    return jnp.concatenate([y_p, y_s], axis=0), _ssd_state_from_kernel(st)


HG_HALF = HG_CHUNK // 2
HG_UNROLL = 2


def _block_cumsum(x, row, reverse):
    n = x.shape[0]
    within = row % HG_CHUNK
    step = 1
    while step < HG_CHUNK:
        if reverse:
            shifted = pltpu.roll(x, n - step, 0)
            ok = within < HG_CHUNK - step
        else:
            shifted = pltpu.roll(x, step, 0)
            ok = within >= step
        x = x + jnp.where(ok, shifted, 0.0)
        step *= 2
    return x


def _hg_kernel(*refs, length, has_init, emit_state):
    q_ref, ff_ref, fb_ref, v_ref, gate_ref, lb_ref, gn_ref = refs[:7]
    pos = 7
    if has_init:
        s0_ref = refs[pos]
        pos += 1
    o_ref = refs[pos]
    pos += 1
    if emit_state:
        so_ref = refs[pos]
        pos += 1
    qs_s, g_s, k_s, o_s, st_s = refs[pos:pos + 5]

    row = lax.broadcasted_iota(jnp.int32, (length, HG_DK), 0)
    sub = lax.broadcasted_iota(jnp.int32, (HG_HALF, HG_DK), 0)
    qs_s[...] = _silu(q_ref[...])
    n_blocks = length // HG_CHUNK

    for d, f_ref in enumerate((ff_ref, fb_ref)):
        lb = lb_ref[d]
        f = f_ref[...]
        log_sig = jnp.minimum(f, 0.0) - jnp.log1p(jnp.exp(-jnp.abs(f)))
        a = jnp.log(lb)
        b = jnp.log1p(-lb) + log_sig
        log_g = jnp.maximum(a, b) + jnp.log1p(jnp.exp(-jnp.abs(a - b)))
        g_s[d] = _block_cumsum(log_g, row, reverse=(d == 1))
        k_s[d] = (1.0 - lb) * jax.nn.sigmoid(-f)
        if has_init:
            st_s[d] = s0_ref[d].T
        else:
            st_s[d] = jnp.zeros((HG_DV, HG_DK), F32)

    def block(ci, carry):
        for d in range(2):
            c = ci if d == 0 else n_blocks - 1 - ci
            rows = pl.ds(pl.multiple_of(c * HG_CHUNK, HG_CHUNK), HG_CHUNK)
            gc = g_s[d, rows, :]
            q = qs_s[rows, :]
            k = k_s[d, rows, :]
            v = v_ref[rows, :]
            st = st_s[d]
            g_end = gc[HG_CHUNK - 1:HG_CHUNK, :] if d == 0 else gc[0:1, :]
            o = _nt_dot((q * jnp.exp(gc)).astype(BF16), st.astype(BF16))
            halves = []
            for hh in range(2):
                gh = gc[hh * HG_HALF:(hh + 1) * HG_HALF]
                qh = q[hh * HG_HALF:(hh + 1) * HG_HALF]
                acc = o[hh * HG_HALF:(hh + 1) * HG_HALF]
                for s in range(HG_CHUNK):
                    lo, hi = hh * HG_HALF, (hh + 1) * HG_HALF - 1
                    if (d == 0 and s > hi) or (d == 1 and s < lo):
                        continue
                    w = qh * k[s:s + 1] * jnp.exp(jnp.minimum(gh - gc[s:s + 1], 0.0))
                    if d == 0 and s > lo:
                        w = jnp.where(sub >= s - lo, w, 0.0)
                    if d == 1 and s < hi:
                        w = jnp.where(sub <= s - lo, w, 0.0)
                    acc = acc + jnp.sum(w, axis=-1, keepdims=True) * v[s:s + 1]
                halves.append(acc)
            o_s[d, rows, :] = jnp.concatenate(halves, axis=0)
            k_hat = k * jnp.exp(g_end - gc)
            upd = lax.dot_general(v.astype(BF16), k_hat.astype(BF16), (((0,), (0,)), ((), ())),
                                  preferred_element_type=F32)
            st_s[d] = st * jnp.exp(g_end) + upd
        return carry

    lax.fori_loop(0, n_blocks, block, 0, unroll=HG_UNROLL)
    if emit_state:
        for d in range(2):
            so_ref[d] = st_s[d].T

    o = o_s[0] + o_s[1]
    o = o * lax.rsqrt(jnp.mean(o * o, axis=-1, keepdims=True) + EPS) * gn_ref[...]
    o_ref[...] = (o * _silu(gate_ref[...])).astype(o_ref.dtype)


def hgrn2_scan(proj, row0, n_seq, length, lb, g_norm, wl, s0, emit_state):
    b0 = row0 // length
    col = lambda base: pl.BlockSpec((length, HG_DK), lambda b, h: (b0 + b, base * HG_HEADS + h))
    in_specs = [col(0), col(1), col(2), col(3), col(4),
                pl.BlockSpec((2, 1, HG_DK), lambda b, h: (0, 0, h)),
                pl.BlockSpec((None, 1, HG_DV), lambda b, h: (wl, 0, 0))]
    args = [proj, proj, proj, proj, proj, lb, g_norm]
    state_spec = pl.BlockSpec((None, 2, None, HG_DK, HG_DV), lambda b, h: (b, 0, h, 0, 0))
    if s0 is not None:
        in_specs.append(state_spec)
        args.append(s0)
    out_specs = [pl.BlockSpec((length, HG_DV), lambda b, h: (b, h))]
    out_shape = [jax.ShapeDtypeStruct((n_seq * length, D_MODEL), BF16)]
    if emit_state:
        out_specs.append(state_spec)
        out_shape.append(jax.ShapeDtypeStruct((n_seq, 2, HG_HEADS, HG_DK, HG_DV), F32))
    return pl.pallas_call(
        functools.partial(_hg_kernel, length=length, has_init=s0 is not None, emit_state=emit_state),
        grid=(n_seq, HG_HEADS),
        in_specs=in_specs,
        out_specs=out_specs,
        out_shape=out_shape,
        scratch_shapes=[pltpu.VMEM((length, HG_DK), F32), pltpu.VMEM((2, length, HG_DK), F32),
                        pltpu.VMEM((2, length, HG_DK), F32), pltpu.VMEM((2, length, HG_DV), F32),
                        pltpu.VMEM((2, HG_DV, HG_DK), F32)],
        compiler_params=_cparams(("arbitrary", "arbitrary")),
        name="hgrn2_scan",
    )(*args)


def hgrn2_mixer(h, wl, w_in, lb, g_norm, state_in):
    proj = matmul(h, w_in, wl)
    lb3 = lb.reshape(2, 1, HG_FDIM)
    gn3 = g_norm.reshape(-1, 1, HG_DV)
    o_p, st = hgrn2_scan(proj, 0, BATCH, SEQ, lb3, gn3, wl, None, True)
    (o_s,) = hgrn2_scan(proj, T_PROMPT, DEC_BATCH, DEC_SEQ, lb3, gn3, wl, state_in, False)
    return jnp.concatenate([o_p, o_s], axis=0), st


def _final_norm_kernel(x_ref, g_ref, o_ref):
    x = x_ref[...]
    o_ref[...] = x * lax.rsqrt(jnp.mean(x * x, axis=-1, keepdims=True) + EPS) * g_ref[...]


def final_norm(x, gain):
    t = x.shape[0]
    return pl.pallas_call(
        _final_norm_kernel,
        grid=(t // NORM_TM,),
        in_specs=[pl.BlockSpec((NORM_TM, D_MODEL), lambda i: (i, 0)), pl.BlockSpec((1, D_MODEL), lambda i: (0, 0))],
        out_specs=pl.BlockSpec((NORM_TM, D_MODEL), lambda i: (i, 0)),
        out_shape=jax.ShapeDtypeStruct((t, D_MODEL), F32),
        compiler_params=_cparams(("arbitrary",)),
        name="final_norm",
    )(x, gain.reshape(1, D_MODEL))


def kernel(x_prompt, x_sample, cache_attn_k, cache_attn_v, state_hgrn, state_ssd, c, c_ctx, ada_w, ada_b, norm_mix, norm_ffn, norm_final, hy_w_in, hy_w_short, hy_filt_w1, hy_filt_b1, hy_filt_w_hid, hy_filt_b_hid, hy_filt_freq, hy_filt_w_out, hy_skip, hy_w_out, hg_w_in, hg_lb, hg_norm, hg_w_o, ssd_w_in, ssd_conv_w, ssd_conv_b, ssd_dt_bias, ssd_a_log, ssd_d, ssd_norm, ssd_w_out, at_w_qkv, at_sink, at_w_o, moe_w_router, moe_b_router, moe_w_in, moe_b_in, moe_w_out, moe_b_out):
    x = jnp.concatenate([x_prompt.reshape(T_PROMPT, D_MODEL), x_sample.reshape(T_SAMPLE, D_MODEL)], axis=0)
    cvec = jnp.zeros((N_CVEC, D_MODEL), F32).at[0].set(c_ctx).at[1:1 + DEC_BATCH].set(c)
    mods = ada_mods(cvec, ada_w, ada_b)
    lbs = jax.nn.softmax(hg_lb.astype(F32), axis=1)
    lbs = jnp.cumsum(lbs, axis=1) - lbs[:, :1]
    w_router_t = moe_w_router.transpose(0, 2, 1)
    new_k, new_v, new_hg, new_ssd = [], [], [], []
    for li in range(DEPTH):
        kind, j = li % N_MIXERS, li // N_MIXERS
        h = norm_modulate(x, norm_mix, mods, li, 0)
        if kind == 0:
            y = hyena_mixer(h, j, hy_w_in, hy_w_short, hy_filt_w1, hy_filt_b1, hy_filt_w_hid, hy_filt_b_hid,
                            hy_filt_freq, hy_filt_w_out, hy_skip)
            w_o = hy_w_out
        elif kind == 1:
            y, st = hgrn2_mixer(h, j, hg_w_in, lbs[:, li], hg_norm, state_hgrn[:, j])
            new_hg.append(st)
            w_o = hg_w_o
        elif kind == 2:
            y, st = ssd_mixer(h, j, ssd_w_in, ssd_conv_w, ssd_conv_b, ssd_dt_bias, ssd_a_log, ssd_d, ssd_norm,
                              state_ssd[:, j])
            new_ssd.append(st)
            w_o = ssd_w_out
        else:
            y, kc, vc = attention_mixer(h, at_w_qkv, j, at_sink[j], cache_attn_k[:, j], cache_attn_v[:, j])
            new_k.append(kc)
            new_v.append(vc)
            w_o = at_w_o
        x = matmul_residual(y, w_o, j, x, mods, li, 2)
        x = moe_layer(x, li, mods, norm_ffn, w_router_t, moe_b_router, moe_w_in, moe_b_in, moe_w_out, moe_b_out)
    y = final_norm(x, norm_final)
    y_prompt = y[:T_PROMPT].reshape(BATCH, SEQ, D_MODEL)
    y_sample = y[T_PROMPT:].reshape(DEC_BATCH, DEC_SEQ, D_MODEL)
    return (y_prompt, y_sample, jnp.stack(new_k, axis=1), jnp.stack(new_v, axis=1),
            jnp.stack(new_hg, axis=1), jnp.stack(new_ssd, axis=1))
```

```python
import functools
import math

import numpy as np
import jax
import jax.numpy as jnp
from jax import lax
from jax.experimental import pallas as pl
from jax.experimental.pallas import tpu as pltpu

F32 = jnp.float32
BF16 = jnp.bfloat16
HIGHEST = lax.Precision.HIGHEST

D_MODEL = 2048
BATCH = 32
SEQ = 256
DEPTH = 4
DEC_BATCH = 2
DEC_SEQ = 1024
PAST_LEN = 256
GRID_W = 64
N_MIXERS = 4
N_MOD = 6
EPS = 1e-6

HY_ORDER = 2
HY_SHORT_W = 3
HY_EMB_DIM = 33
HY_BANDS = (HY_EMB_DIM - 1) // 2
HY_FILTER_W = 64
HY_N_SIN = 3
HY_DECAY_TARGET = 1e-2
HY_FAST_DECAY = 0.3
HY_SLOW_DECAY = 1.5

HG_HEADS = D_MODEL // 128
HG_DK = 128
HG_DV = D_MODEL // HG_HEADS
HG_FDIM = HG_HEADS * HG_DK
HG_IN_DIM = 3 * HG_FDIM + 2 * D_MODEL
HG_CHUNK = 16

SSD_INNER = 2 * D_MODEL
SSD_HEADDIM = 64
SSD_HEADS = SSD_INNER // SSD_HEADDIM
SSD_GROUPS = 8
SSD_STATE = 128
SSD_CONV_W = 5
SSD_GN = SSD_GROUPS * SSD_STATE
SSD_CONV_DIM = SSD_INNER + 2 * SSD_GN
SSD_IN_DIM = SSD_INNER + SSD_CONV_DIM + 2 * SSD_HEADS

AT_HEAD_DIM = 64
AT_HEADS = D_MODEL // AT_HEAD_DIM
AT_KV_HEADS = AT_HEADS // 8
AT_REP = AT_HEADS // AT_KV_HEADS
AT_WINDOW = 128
AT_BLOCK = 128
ROPE_THETA = 10000.0

N_EXPERTS = 32
TOP_K = 4
D_EXPERT = D_MODEL
SWIGLU_ALPHA = 1.702
SWIGLU_LIMIT = 7.0

T_PROMPT = BATCH * SEQ
T_SAMPLE = DEC_BATCH * DEC_SEQ
T_ALL = T_PROMPT + T_SAMPLE
N_CVEC = 8

VMEM_LIMIT = 56 * 1024 * 1024


def _cparams(sem):
    return pltpu.CompilerParams(dimension_semantics=sem, vmem_limit_bytes=VMEM_LIMIT)


def _mod_row(tile_idx, tile_rows):
    start = tile_idx * tile_rows
    return jnp.where(start < T_PROMPT, 0, 1 + (start - T_PROMPT) // DEC_SEQ)


ADA_TN = 1024


def _ada_kernel(c_ref, w_ref, b_ref, o_ref):
    c = c_ref[...]
    s = c * jax.nn.sigmoid(c)
    o_ref[...] = jnp.dot(s.astype(BF16), w_ref[...].astype(BF16), preferred_element_type=F32) + b_ref[...]


def ada_mods(cvec, ada_w, ada_b):
    n = N_MOD * D_MODEL
    out = pl.pallas_call(
        _ada_kernel,
        grid=(DEPTH, n // ADA_TN),
        in_specs=[
            pl.BlockSpec((N_CVEC, D_MODEL), lambda l, j: (0, 0)),
            pl.BlockSpec((None, D_MODEL, ADA_TN), lambda l, j: (l, 0, j)),
            pl.BlockSpec((None, 1, ADA_TN), lambda l, j: (l, 0, j)),
        ],
        out_specs=pl.BlockSpec((None, N_CVEC, ADA_TN), lambda l, j: (l, 0, j)),
        out_shape=jax.ShapeDtypeStruct((DEPTH, N_CVEC, n), F32),
        compiler_params=_cparams(("arbitrary", "arbitrary")),
        name="ada_mods",
    )(cvec, ada_w, ada_b.reshape(DEPTH, 1, n))
    return out.reshape(DEPTH, N_CVEC * N_MOD, 1, D_MODEL)


def _mod_spec(li, comp, tile_rows):
    return pl.BlockSpec((None, None, 1, D_MODEL),
                        lambda i, *_: (li, _mod_row(i, tile_rows) * N_MOD + comp, 0, 0))


NORM_TM = 256


def _norm_mod(x, g, sh, sc):
    y = x * lax.rsqrt(jnp.mean(x * x, axis=-1, keepdims=True) + EPS) * g
    return y * (1.0 + sc) + sh


def _norm_kernel(x_ref, g_ref, sh_ref, sc_ref, o_ref):
    o_ref[...] = _norm_mod(x_ref[...], g_ref[...], sh_ref[...], sc_ref[...]).astype(o_ref.dtype)


def norm_modulate(x, gains, mods, li, shift_comp):
    t = x.shape[0]
    return pl.pallas_call(
        _norm_kernel,
        grid=(t // NORM_TM,),
        in_specs=[
            pl.BlockSpec((NORM_TM, D_MODEL), lambda i: (i, 0)),
            pl.BlockSpec((None, 1, D_MODEL), lambda i: (li, 0, 0)),
            _mod_spec(li, shift_comp, NORM_TM),
            _mod_spec(li, shift_comp + 1, NORM_TM),
        ],
        out_specs=pl.BlockSpec((NORM_TM, D_MODEL), lambda i: (i, 0)),
        out_shape=jax.ShapeDtypeStruct((t, D_MODEL), BF16),
        compiler_params=_cparams(("arbitrary",)),
        name="norm_modulate",
    )(x, gains.reshape(DEPTH, 1, D_MODEL), mods, mods)


def _norm_router_kernel(x_ref, g_ref, sh_ref, sc_ref, wr_ref, br_ref, o_ref, ti_ref, tg_ref):
    h = _norm_mod(x_ref[...], g_ref[...], sh_ref[...], sc_ref[...])
    o_ref[...] = _pack_bf16_pairs(h)
    logits = lax.dot_general(wr_ref[...], h, (((1,), (1,)), ((), ())),
                             precision=HIGHEST, preferred_element_type=F32) + br_ref[...]
    eidx = lax.broadcasted_iota(jnp.int32, logits.shape, 0)
    vals = logits
    top_v = []
    for k in range(TOP_K):
        m = jnp.max(vals, axis=0, keepdims=True)
        sel = jnp.min(jnp.where(vals == m, eidx, N_EXPERTS), axis=0, keepdims=True)
        top_v.append(m)
        ti_ref[k:k + 1, :] = sel
        vals = jnp.where(eidx == sel, -jnp.inf, vals)
    ex = [jnp.exp(v - top_v[0]) for v in top_v]
    den = ex[0] + ex[1] + ex[2] + ex[3]
    for k in range(TOP_K):
        tg_ref[k:k + 1, :] = ex[k] / den


def norm_router(x, gains, mods, li, w_router_t, b_router):
    t = x.shape[0]
    return pl.pallas_call(
        _norm_router_kernel,
        grid=(t // NORM_TM,),
        in_specs=[
            pl.BlockSpec((NORM_TM, D_MODEL), lambda i: (i, 0)),
            pl.BlockSpec((None, 1, D_MODEL), lambda i: (li, 0, 0)),
            _mod_spec(li, 3, NORM_TM),
            _mod_spec(li, 4, NORM_TM),
            pl.BlockSpec((None, N_EXPERTS, D_MODEL), lambda i: (li, 0, 0)),
            pl.BlockSpec((None, N_EXPERTS, 1), lambda i: (li, 0, 0)),
        ],
        out_specs=[
            pl.BlockSpec((NORM_TM, D_MODEL // 2), lambda i: (i, 0)),
            pl.BlockSpec((TOP_K, NORM_TM), lambda i: (0, i)),
            pl.BlockSpec((TOP_K, NORM_TM), lambda i: (0, i)),
        ],
        out_shape=[
            jax.ShapeDtypeStruct((t, D_MODEL // 2), jnp.uint32),
            jax.ShapeDtypeStruct((TOP_K, t), jnp.int32),
            jax.ShapeDtypeStruct((TOP_K, t), F32),
        ],
        compiler_params=_cparams(("arbitrary",)),
        name="norm_router",
    )(x, gains.reshape(DEPTH, 1, D_MODEL), mods, mods, w_router_t, b_router.reshape(DEPTH, N_EXPERTS, 1))


MM_TM = 1024


def _mm_kernel(x_ref, w_ref, o_ref, wb_ref):
    @pl.when(pl.program_id(1) == 0)
    def _():
        wb_ref[...] = w_ref[...].astype(BF16)

    o_ref[...] = jnp.dot(x_ref[...], wb_ref[...], preferred_element_type=F32).astype(o_ref.dtype)


def _mm_res_kernel(x_ref, w_ref, r_ref, g_ref, o_ref, wb_ref):
    @pl.when(pl.program_id(1) == 0)
    def _():
        wb_ref[...] = w_ref[...].astype(BF16)

    o_ref[...] = r_ref[...] + g_ref[...] * jnp.dot(x_ref[...], wb_ref[...], preferred_element_type=F32)


def _w_spec(w, wl, k, tn):
    if w.ndim == 3:
        return pl.BlockSpec((None, k, tn), lambda j, i: (wl, 0, j))
    return pl.BlockSpec((k, tn), lambda j, i: (0, j))


def matmul(x, w, wl=0, out_dtype=F32, tn=1024):
    t, k = x.shape
    n = w.shape[-1]
    assert n % tn == 0 and t % MM_TM == 0
    return pl.pallas_call(
        _mm_kernel,
        grid=(n // tn, t // MM_TM),
        in_specs=[pl.BlockSpec((MM_TM, k), lambda j, i: (i, 0)), _w_spec(w, wl, k, tn)],
        out_specs=pl.BlockSpec((MM_TM, tn), lambda j, i: (i, j)),
        out_shape=jax.ShapeDtypeStruct((t, n), out_dtype),
        scratch_shapes=[pltpu.VMEM((k, tn), BF16)],
        compiler_params=_cparams(("arbitrary", "arbitrary")),
        name="matmul",
    )(x, w)


def matmul_residual(x, w, wl, res, mods, li, gate_comp, tn=512):
    t, k = x.shape
    n = w.shape[-1]
    assert n % tn == 0 and t % MM_TM == 0
    gate_spec = pl.BlockSpec((None, None, 1, tn),
                             lambda j, i: (li, _mod_row(i, MM_TM) * N_MOD + gate_comp, 0, j))
    return pl.pallas_call(
        _mm_res_kernel,
        grid=(n // tn, t // MM_TM),
        in_specs=[pl.BlockSpec((MM_TM, k), lambda j, i: (i, 0)), _w_spec(w, wl, k, tn),
                  pl.BlockSpec((MM_TM, tn), lambda j, i: (i, j)), gate_spec],
        out_specs=pl.BlockSpec((MM_TM, tn), lambda j, i: (i, j)),
        out_shape=jax.ShapeDtypeStruct((t, n), F32),
        scratch_shapes=[pltpu.VMEM((k, tn), BF16)],
        compiler_params=_cparams(("arbitrary", "arbitrary")),
        name="matmul_residual",
    )(x, w, res, mods)


MOE_BLK = 256
MOE_SB_BLKS = 6
MOE_M_BLKS = (3, 4, 5, 6)
MOE_RB = MOE_BLK * MOE_SB_BLKS
MOE_TH = 256
MOE_NJ = D_EXPERT // MOE_TH
D_PACK = D_MODEL // 2


def _moe_nblk(t):
    return pl.cdiv(t * TOP_K, MOE_BLK) + N_EXPERTS


def _moe_nsb(t):
    return pl.cdiv(t * TOP_K, MOE_RB) + N_EXPERTS


def moe_dispatch(top_i):
    t = top_i.shape[1]
    moe_rows, moe_nsb = _moe_nblk(t) * MOE_BLK, _moe_nsb(t)
    flat_e = top_i.T.reshape(-1)
    onehot = (flat_e[:, None] == jnp.arange(N_EXPERTS, dtype=jnp.int32)[None, :]).astype(jnp.int32)
    csum = jnp.cumsum(onehot, axis=0)
    rank = jnp.take_along_axis(csum, flat_e[:, None], axis=1)[:, 0] - 1
    counts = csum[-1]
    nblk = (counts + MOE_BLK - 1) // MOE_BLK
    blk_end = jnp.cumsum(nblk)
    blk_start = blk_end - nblk
    pos = blk_start[flat_e] * MOE_BLK + rank
    tok_buf = jnp.zeros((moe_rows,), jnp.int32).at[pos].set(jnp.arange(t * TOP_K, dtype=jnp.int32) // TOP_K)
    n_blk = blk_end[-1]
    nsb_e = (nblk + MOE_SB_BLKS - 1) // MOE_SB_BLKS
    sb_end = jnp.cumsum(nsb_e)
    sb_start = sb_end - nsb_e
    n_sb = sb_end[-1]
    sb = jnp.arange(moe_nsb, dtype=jnp.int32)
    e_of = jnp.minimum(jnp.searchsorted(sb_end, sb, side='right'), N_EXPERTS - 1).astype(jnp.int32)
    local = sb - sb_start[e_of]
    valid = sb < n_sb
    e_last = e_of[jnp.maximum(n_sb - 1, 0)]
    sb_e = jnp.where(valid, e_of, e_last).astype(jnp.int32)
    sb_blk0 = jnp.where(valid, blk_start[e_of] + local * MOE_SB_BLKS, 0).astype(jnp.int32)
    sb_nb = jnp.where(valid, jnp.clip(nblk[e_of] - local * MOE_SB_BLKS, 0, MOE_SB_BLKS), 0).astype(jnp.int32)
    return dict(pos=pos.astype(jnp.int32), tok_blk=tok_buf.reshape(moe_rows // MOE_BLK, 1, MOE_BLK),
                sb_e=sb_e, sb_blk0=sb_blk0, sb_nb=sb_nb, n_sb=n_sb.reshape(1).astype(jnp.int32))


def _pack_bf16_pairs(h):
    half = h.shape[-1] // 2
    lo = lax.bitcast_convert_type(h[:, :half].astype(BF16).astype(F32), jnp.uint32)
    hi = lax.bitcast_convert_type(h[:, half:].astype(BF16).astype(F32), jnp.uint32)
    return (hi & jnp.uint32(0xFFFF0000)) | (lo >> 16)


def _unpack_bf16_pairs(p):
    lo = lax.bitcast_convert_type(p << 16, F32).astype(BF16)
    hi = lax.bitcast_convert_type(p & jnp.uint32(0xFFFF0000), F32).astype(BF16)
    return lo, hi


MOE_DMA_UNROLL = 8


def _moe_kernel(sbe_ref, sbb_ref, sbn_ref, nsb_ref, *refs):
    tok_refs = refs[:MOE_SB_BLKS]
    (hp_ref, wg_ref, wu_ref, bg_ref, bu_ref, wo_ref, bo_ref,
     ys_ref,
     stage, xb_buf, y_acc, sem_in, sem_out) = refs[MOE_SB_BLKS:]
    sb = pl.program_id(0)
    j = pl.program_id(1)
    nb = sbn_ref[sb]
    row0 = sbb_ref[sb] * MOE_BLK

    def blk(rb):
        return pl.ds(rb * MOE_BLK, MOE_BLK)

    def for_block_rows(rb, fn):
        def body(i, c):
            fn(pltpu.make_async_copy(hp_ref.at[pl.ds(tok_refs[rb][0, i], 1)],
                                     stage.at[rb % 2, pl.ds(i, 1)], sem_in.at[rb % 2]))
            return c
        lax.fori_loop(0, MOE_BLK, body, 0, unroll=MOE_DMA_UNROLL)

    def unpack(rb):
        lo, hi = _unpack_bf16_pairs(stage[rb % 2])
        xb_buf[blk(rb), :D_PACK] = lo
        xb_buf[blk(rb), D_PACK:] = hi

    def out_copy(rb, first_row):
        rows = pl.ds(pl.multiple_of(rb * MOE_BLK, MOE_BLK), MOE_BLK)
        return pltpu.make_async_copy(y_acc.at[rows], ys_ref.at[pl.ds(first_row + rb * MOE_BLK, MOE_BLK)], sem_out)

    def for_out_copies(n, first_row, fn):
        def body(rb, c):
            fn(out_copy(rb, first_row))
            return c
        lax.fori_loop(0, n, body, 0)

    @pl.when(jnp.logical_and(sb > 0, j == 0))
    def _():
        prev = jnp.maximum(sb - 1, 0)
        for_out_copies(sbn_ref[prev], sbb_ref[prev] * MOE_BLK, lambda cp: cp.wait())

    @pl.when(jnp.logical_and(nb > 0, j == 0))
    def _():
        for rb in range(MOE_SB_BLKS + 1):
            if rb < MOE_SB_BLKS:
                pl.when(rb < nb)(functools.partial(for_block_rows, rb, lambda cp: cp.start()))
            if rb >= 1:
                @pl.when(rb - 1 < nb)
                def _(rb=rb):
                    for_block_rows(rb - 1, lambda cp: cp.wait())
                    unpack(rb - 1)
        for rb in range(1, MOE_M_BLKS[0]):
            @pl.when(nb <= rb)
            def _(rb=rb):
                xb_buf[blk(rb), :] = jnp.zeros((MOE_BLK, D_MODEL), BF16)
        y_acc[...] = jnp.broadcast_to(bo_ref[...], y_acc.shape)

    def ffn(m_blks):
        m = m_blks * MOE_BLK
        x = xb_buf[:m, :]
        hg = jnp.dot(x, wg_ref[...].astype(BF16), preferred_element_type=F32) + bg_ref[...]
        hu = jnp.dot(x, wu_ref[...].astype(BF16), preferred_element_type=F32) + bu_ref[...]
        g = jnp.minimum(hg, SWIGLU_LIMIT)
        u = jnp.clip(hu, -SWIGLU_LIMIT, SWIGLU_LIMIT)
        a = (u + 1.0) * g * jax.nn.sigmoid(SWIGLU_ALPHA * g)
        y_acc[:m, :] += jnp.dot(a.astype(BF16), wo_ref[...].astype(BF16), preferred_element_type=F32)

    for idx, m_blks in enumerate(MOE_M_BLKS):
        if idx == 0:
            cond = jnp.logical_and(nb > 0, nb <= m_blks)
        else:
            cond = nb == m_blks
        pl.when(cond)(functools.partial(ffn, m_blks))

    @pl.when(j == MOE_NJ - 1)
    def _():
        for_out_copies(nb, row0, lambda cp: cp.start())

        @pl.when(sb == pl.num_programs(0) - 1)
        def _():
            for_out_copies(nb, row0, lambda cp: cp.wait())


def moe_experts(hp, tables, w_in, b_in, w_out, b_out, li):
    n_rows = _moe_nblk(hp.shape[0]) * MOE_BLK

    def live_j(sb, j, nsb):
        return jnp.where(sb < nsb[0], j, MOE_NJ - 1)

    n_blk_max = tables["tok_blk"].shape[0]
    tok_spec = lambda rb: pl.BlockSpec(
        (None, 1, MOE_BLK), lambda sb, j, sbe, sbb, sbn, nsb: (jnp.minimum(sbb[sb] + rb, n_blk_max - 1), 0, 0),
        memory_space=pltpu.SMEM)

    w_in_spec = lambda half: pl.BlockSpec(
        (None, None, D_MODEL, MOE_TH),
        lambda sb, j, sbe, sbb, sbn, nsb: (li, sbe[sb], 0, half * MOE_NJ + live_j(sb, j, nsb)))
    b_in_spec = lambda half: pl.BlockSpec(
        (None, None, 1, MOE_TH),
        lambda sb, j, sbe, sbb, sbn, nsb: (li, sbe[sb], 0, half * MOE_NJ + live_j(sb, j, nsb)))
    return pl.pallas_call(
        _moe_kernel,
        grid_spec=pltpu.PrefetchScalarGridSpec(
            num_scalar_prefetch=4,
            grid=(tables["sb_e"].shape[0], MOE_NJ),
            in_specs=[tok_spec(rb) for rb in range(MOE_SB_BLKS)] + [
                pl.BlockSpec(memory_space=pl.ANY),
                w_in_spec(0), w_in_spec(1), b_in_spec(0), b_in_spec(1),
                pl.BlockSpec((None, None, MOE_TH, D_MODEL),
                             lambda sb, j, sbe, sbb, sbn, nsb: (li, sbe[sb], live_j(sb, j, nsb), 0)),
                pl.BlockSpec((None, None, 1, D_MODEL), lambda sb, j, sbe, sbb, sbn, nsb: (li, sbe[sb], 0, 0)),
            ],
            out_specs=pl.BlockSpec(memory_space=pl.ANY),
            scratch_shapes=[
                pltpu.VMEM((2, MOE_BLK, D_PACK), jnp.uint32),
                pltpu.VMEM((MOE_RB, D_MODEL), BF16),
                pltpu.VMEM((MOE_RB, D_MODEL), F32),
                pltpu.SemaphoreType.DMA((2,)),
                pltpu.SemaphoreType.DMA(()),
            ],
        ),
        out_shape=jax.ShapeDtypeStruct((n_rows, D_MODEL), F32),
        compiler_params=_cparams(("arbitrary", "arbitrary")),
        name="moe_experts",
    )(tables["sb_e"], tables["sb_blk0"], tables["sb_nb"], tables["n_sb"],
      *([tables["tok_blk"]] * MOE_SB_BLKS), hp, w_in, w_in, b_in.reshape(-1, N_EXPERTS, 1, 2 * D_EXPERT), b_in.reshape(-1, N_EXPERTS, 1, 2 * D_EXPERT),
      w_out, b_out.reshape(-1, N_EXPERTS, 1, D_MODEL))


CMB_TM = 128


def _combine_kernel(pos_ref, posn_ref, ys_ref, x_ref, tg_ref, mg_ref, o_ref, buf, sems):
    i = pl.program_id(0)
    nt = pl.num_programs(0)
    slot = i % 2

    def for_rows(idx_ref, dst_slot, fn):
        for k in range(TOP_K):
            def body(r, c, k=k):
                fn(pltpu.make_async_copy(ys_ref.at[pl.ds(idx_ref[0, k * CMB_TM + r], 1)],
                                         buf.at[dst_slot, k, pl.ds(r, 1)], sems.at[dst_slot]))
                return c
            lax.fori_loop(0, CMB_TM, body, 0, unroll=MOE_DMA_UNROLL)

    @pl.when(i == 0)
    def _():
        for_rows(pos_ref, 0, lambda cp: cp.start())

    @pl.when(i + 1 < nt)
    def _():
        for_rows(posn_ref, 1 - slot, lambda cp: cp.start())

    for_rows(pos_ref, slot, lambda cp: cp.wait())
    tg = tg_ref[...]
    y = tg[:, 0:1] * buf[slot, 0]
    for k in range(1, TOP_K):
        y = y + tg[:, k:k + 1] * buf[slot, k]
    o_ref[...] = x_ref[...] + mg_ref[...] * y


def moe_combine(ys, pos, gates_t, x, mods, li):
    t = x.shape[0]
    nt = t // CMB_TM
    pos_tiles = pos.reshape(nt, CMB_TM, TOP_K).transpose(0, 2, 1).reshape(nt, 1, TOP_K * CMB_TM)
    pos_spec = lambda shift: pl.BlockSpec((None, 1, TOP_K * CMB_TM),
                                          lambda i: (jnp.minimum(i + shift, nt - 1), 0, 0), memory_space=pltpu.SMEM)
    return pl.pallas_call(
        _combine_kernel,
        grid=(nt,),
        in_specs=[
            pos_spec(0), pos_spec(1),
            pl.BlockSpec(memory_space=pl.ANY),
            pl.BlockSpec((CMB_TM, D_MODEL), lambda i: (i, 0)),
            pl.BlockSpec((CMB_TM, TOP_K), lambda i: (i, 0)),
            _mod_spec(li, 5, CMB_TM),
        ],
        out_specs=pl.BlockSpec((CMB_TM, D_MODEL), lambda i: (i, 0)),
        out_shape=jax.ShapeDtypeStruct((t, D_MODEL), F32),
        scratch_shapes=[pltpu.VMEM((2, TOP_K, CMB_TM, D_MODEL), F32), pltpu.SemaphoreType.DMA((2,))],
        compiler_params=_cparams(("arbitrary",)),
        name="moe_combine",
    )(pos_tiles, pos_tiles, ys, x, gates_t, mods)


def moe_layer(x, li, mods, norm_ffn, w_router_t, b_router, w_in, b_in, w_out, b_out):
    hp, top_i, top_g = norm_router(x, norm_ffn, mods, li, w_router_t, b_router)
    tables = moe_dispatch(top_i)
    ys = moe_experts(hp, tables, w_in, b_in, w_out, b_out, li)
    return moe_combine(ys, tables["pos"], top_g.T, x, mods, li)


AT_NH = AT_HEADS + 2 * AT_KV_HEADS
AT_SCALE = AT_HEAD_DIM ** -0.5


def _rope_tables(length):
    pos = np.arange(length)
    half, quarter = AT_HEAD_DIM // 2, AT_HEAD_DIM // 4
    inv = ROPE_THETA ** (-np.arange(0, half, 2, dtype=np.float64) / half)
    j = np.arange(AT_HEAD_DIM)
    p = np.where(j[None, :] < half, (pos // GRID_W)[:, None], (pos % GRID_W)[:, None]).astype(np.float64)
    ang = p * inv[(j % half) % quarter][None, :]
    first = (j % half) < quarter
    sin_signed = np.where(first[None, :], -np.sin(ang), np.sin(ang))
    partner = np.where(first, j + quarter, j - quarter)
    perm = np.zeros((AT_HEAD_DIM, AT_HEAD_DIM), np.float32)
    perm[partner, j] = 1.0
    return jnp.asarray(np.cos(ang), F32), jnp.asarray(sin_signed, F32), jnp.asarray(perm)


def _rope_kernel(x_ref, cos_ref, sin_ref, perm_ref, o_ref):
    x = x_ref[...]
    xp = jnp.dot(x, perm_ref[...], precision=HIGHEST, preferred_element_type=F32)
    o_ref[...] = x * cos_ref[...] + xp * sin_ref[...]


def rope_heads(xh, n_heads):
    b, _, length, dh = xh.shape
    cos, sin, perm = _rope_tables(length)
    return pl.pallas_call(
        _rope_kernel,
        grid=(b, n_heads),
        in_specs=[
            pl.BlockSpec((None, None, length, dh), lambda i, h: (i, h, 0, 0)),
            pl.BlockSpec((length, dh), lambda i, h: (0, 0)),
            pl.BlockSpec((length, dh), lambda i, h: (0, 0)),
            pl.BlockSpec((dh, dh), lambda i, h: (0, 0)),
        ],
        out_specs=pl.BlockSpec((None, None, length, dh), lambda i, h: (i, h, 0, 0)),
        out_shape=jax.ShapeDtypeStruct((b, n_heads, length, dh), F32),
        compiler_params=_cparams(("arbitrary", "arbitrary")),
        name="rope_heads",
    )(xh, cos, sin, perm)


def _nt_dot(a, b):
    return lax.dot_general(a, b, (((1,), (1,)), ((), ())), preferred_element_type=F32)


def _attn_ctx_kernel(sink_ref, q_ref, k_ref, v_ref, o_ref):
    g = pl.program_id(1)
    k = k_ref[...].astype(BF16)
    v = v_ref[...].astype(BF16)
    for r in range(AT_REP):
        q = (q_ref[r] * AT_SCALE).astype(BF16)
        s = _nt_dot(q, k)
        sk = sink_ref[g * AT_REP + r]
        m = jnp.maximum(jnp.max(s, axis=-1, keepdims=True), sk)
        p = jnp.exp(s - m)
        den = jnp.sum(p, axis=-1, keepdims=True) + jnp.exp(sk - m)
        o = jnp.dot(p.astype(BF16), v, preferred_element_type=F32) / den
        o_ref[r] = o.astype(o_ref.dtype)


def attn_context(qkvh, sink):
    b, _, length, dh = qkvh.shape
    return pl.pallas_call(
        _attn_ctx_kernel,
        grid=(b, AT_KV_HEADS),
        in_specs=[
            pl.BlockSpec(memory_space=pltpu.SMEM),
            pl.BlockSpec((None, AT_REP, length, dh), lambda i, g: (i, g, 0, 0)),
            pl.BlockSpec((None, None, length, dh), lambda i, g: (i, AT_HEADS + g, 0, 0)),
            pl.BlockSpec((None, None, length, dh), lambda i, g: (i, AT_HEADS + AT_KV_HEADS + g, 0, 0)),
        ],
        out_specs=pl.BlockSpec((None, AT_REP, length, dh), lambda i, g: (i, g, 0, 0)),
        out_shape=jax.ShapeDtypeStruct((b, AT_HEADS, length, dh), BF16),
        compiler_params=_cparams(("arbitrary", "arbitrary")),
        name="attn_context",
    )(sink, qkvh, qkvh, qkvh)


def _attn_lat_kernel(sink_ref, q_ref, k0_ref, k1_ref, k2_ref, v0_ref, v1_ref, v2_ref, kc_ref, vc_ref, o_ref, *, length):
    g = pl.program_id(1)
    i = pl.program_id(2)
    kb = jnp.concatenate([k0_ref[...], k1_ref[...], k2_ref[...]], axis=0).astype(BF16)
    vb = jnp.concatenate([v0_ref[...], v1_ref[...], v2_ref[...]], axis=0).astype(BF16)
    kc = kc_ref[...].astype(BF16)
    vc = vc_ref[...].astype(BF16)
    qpos = i * AT_BLOCK + lax.broadcasted_iota(jnp.int32, (AT_BLOCK, 3 * AT_BLOCK), 0)
    kpos = (i - 1) * AT_BLOCK + lax.broadcasted_iota(jnp.int32, (AT_BLOCK, 3 * AT_BLOCK), 1)
    allowed = (jnp.abs(qpos - kpos) <= AT_WINDOW) & (kpos >= 0) & (kpos < length)
    for r in range(AT_REP):
        q = (q_ref[r] * AT_SCALE).astype(BF16)
        s1 = jnp.where(allowed, _nt_dot(q, kb), -jnp.inf)
        s2 = _nt_dot(q, kc)
        sk = sink_ref[g * AT_REP + r]
        m = jnp.maximum(jnp.maximum(jnp.max(s1, axis=-1, keepdims=True), jnp.max(s2, axis=-1, keepdims=True)), sk)
        p1 = jnp.exp(s1 - m)
        p2 = jnp.exp(s2 - m)
        den = jnp.sum(p1, axis=-1, keepdims=True) + jnp.sum(p2, axis=-1, keepdims=True) + jnp.exp(sk - m)
        o = (jnp.dot(p1.astype(BF16), vb, preferred_element_type=F32)
             + jnp.dot(p2.astype(BF16), vc, preferred_element_type=F32)) / den
        o_ref[r] = o.astype(o_ref.dtype)


def attn_latent(qk_rope, qkvh, k_ctx, v_ctx, sink):
    b, _, length, dh = qkvh.shape
    nblk = length // AT_BLOCK
    lc = k_ctx.shape[2]
    band = lambda head0, shift: pl.BlockSpec(
        (None, None, AT_BLOCK, dh), lambda bi, g, i: (bi, head0 + g, jnp.clip(i + shift, 0, nblk - 1), 0))
    ctx = pl.BlockSpec((None, None, lc, dh), lambda bi, g, i: (bi, g, 0, 0))
    return pl.pallas_call(
        functools.partial(_attn_lat_kernel, length=length),
        grid=(b, AT_KV_HEADS, nblk),
        in_specs=[
            pl.BlockSpec(memory_space=pltpu.SMEM),
            pl.BlockSpec((None, AT_REP, AT_BLOCK, dh), lambda bi, g, i: (bi, g, i, 0)),
            band(AT_HEADS, -1), band(AT_HEADS, 0), band(AT_HEADS, 1),
            band(AT_HEADS + AT_KV_HEADS, -1), band(AT_HEADS + AT_KV_HEADS, 0), band(AT_HEADS + AT_KV_HEADS, 1),
            ctx, ctx,
        ],
        out_specs=pl.BlockSpec((None, AT_REP, AT_BLOCK, dh), lambda bi, g, i: (bi, g, i, 0)),
        out_shape=jax.ShapeDtypeStruct((b, AT_HEADS, length, dh), BF16),
        compiler_params=_cparams(("arbitrary", "arbitrary", "arbitrary")),
        name="attn_latent",
    )(sink, qk_rope, qk_rope, qk_rope, qk_rope, qkvh, qkvh, qkvh, k_ctx, v_ctx)


def attention_mixer(h, w_qkv, wl, sink, cache_k, cache_v):
    qkv = matmul(h, w_qkv, wl, tn=512)
    nkv = AT_KV_HEADS * AT_HEAD_DIM
    new_k = qkv[:T_PROMPT, D_MODEL:D_MODEL + nkv].reshape(BATCH, SEQ, AT_KV_HEADS, AT_HEAD_DIM)
    new_v = qkv[:T_PROMPT, D_MODEL + nkv:].reshape(BATCH, SEQ, AT_KV_HEADS, AT_HEAD_DIM)
    qkv_p = qkv[:T_PROMPT].reshape(BATCH, SEQ, AT_NH, AT_HEAD_DIM).transpose(0, 2, 1, 3)
    qkv_s = qkv[T_PROMPT:].reshape(DEC_BATCH, DEC_SEQ, AT_NH, AT_HEAD_DIM).transpose(0, 2, 1, 3)
    o_p = attn_context(qkv_p, sink)
    qk_rope = rope_heads(qkv_s, AT_HEADS + AT_KV_HEADS)
    o_s = attn_latent(qk_rope, qkv_s, cache_k.transpose(0, 2, 1, 3), cache_v.transpose(0, 2, 1, 3), sink)
    o = jnp.concatenate([o_p.transpose(0, 2, 1, 3).reshape(T_PROMPT, D_MODEL),
                         o_s.transpose(0, 2, 1, 3).reshape(T_SAMPLE, D_MODEL)], axis=0)
    return o, new_k, new_v


HY_FEAT_PAD = 128


def _dft_matrices(length):
    n = np.arange(length, dtype=np.float64)
    ang = np.pi * np.outer(n, n) / length
    alt = np.where(n % 2 == 0, 1.0, -1.0)
    fwd_im = -np.sin(ang)
    fwd_im[0, :] = alt
    fwd = np.concatenate([np.cos(ang), fwd_im], axis=0)
    wk = np.where(n == 0, 1.0, 2.0)[None, :] / (2.0 * length)
    inv_im = -wk * np.sin(ang)
    inv_im[:, 0] = alt / (2.0 * length)
    inv = np.concatenate([wk * np.cos(ang), inv_im], axis=1)
    return jnp.asarray(fwd, F32).astype(BF16), jnp.asarray(inv, F32).astype(BF16)


def _hyena_features(length):
    t = np.linspace(0.0, 1.0, length)[:, None]
    om = 2.0 * np.pi * np.arange(length)[:, None] / length
    f = np.linspace(1e-4, HY_BANDS - 1, HY_BANDS)[None, :]
    z = np.concatenate([t, np.cos(f * om), -np.sin(f * om)], axis=-1)
    return jnp.asarray(np.pad(z, ((0, 0), (0, HY_FEAT_PAD - HY_EMB_DIM))), F32)


def _hyena_deltas():
    max_decay = math.log(HY_DECAY_TARGET) / HY_FAST_DECAY
    min_decay = math.log(HY_DECAY_TARGET) / HY_SLOW_DECAY
    return jnp.asarray(np.abs(np.linspace(min_decay, max_decay, D_MODEL)), F32).reshape(1, D_MODEL)


def _hy_filter_kernel(z_ref, w1_ref, b1_ref, wh_ref, bh_ref, fr_ref, wo00, wo01, wo10, wo11, dl_ref, fwd_ref, k_ref,
                      *, length):
    hp = functools.partial(jnp.dot, precision=HIGHEST, preferred_element_type=F32)
    h = jnp.sin(fr_ref[0] * (hp(z_ref[...], w1_ref[...]) + b1_ref[...]))
    for s in range(HY_N_SIN - 1):
        h = jnp.sin(fr_ref[s + 1] * (hp(h, wh_ref[s]) + bh_ref[s]))
    td = dl_ref.shape[-1]
    row = lax.broadcasted_iota(jnp.int32, (length, td), 0)
    t = row.astype(F32) * (1.0 / (length - 1))
    decay = jnp.exp(-t * dl_ref[...])
    fwd = fwd_ref[...]
    for o, (wf, wb) in enumerate(((wo00, wo01), (wo10, wo11))):
        hf = hp(h, wf[...]) * decay
        hb = jnp.where(row == 0, 0.0, hp(h, wb[...]) * decay)
        ks = jnp.dot(fwd, (hf + hb).astype(BF16), preferred_element_type=F32)
        kd = jnp.dot(fwd[length:], (hf - hb).astype(BF16), preferred_element_type=F32)
        k_ref[o, :length, :] = ks[:length]
        k_ref[o, length:, :] = jnp.where(row == 0, ks[length:length + 1], kd)


def hyena_filter_spectrum(length, wl, w1, b1, w_hid, b_hid, freq, w_filt_out, td=512):
    fwd, _ = _dft_matrices(length)
    nj = D_MODEL // td
    w1p = jnp.pad(w1, ((0, 0), (0, HY_FEAT_PAD - HY_EMB_DIM), (0, 0)))
    n_hy = w1.shape[0]
    wout_spec = lambda c: pl.BlockSpec((None, HY_FILTER_W, td), lambda j: (wl, 0, c * nj + j))
    whole = lambda shape: pl.BlockSpec((None,) + shape, lambda j: (wl,) + (0,) * len(shape))
    return pl.pallas_call(
        functools.partial(_hy_filter_kernel, length=length),
        grid=(nj,),
        in_specs=[
            pl.BlockSpec((length, HY_FEAT_PAD), lambda j: (0, 0)),
            whole((HY_FEAT_PAD, HY_FILTER_W)), whole((1, HY_FILTER_W)),
            whole((HY_N_SIN - 1, HY_FILTER_W, HY_FILTER_W)), whole((HY_N_SIN - 1, 1, HY_FILTER_W)),
            whole((HY_N_SIN, 1, HY_FILTER_W)),
            wout_spec(0), wout_spec(1), wout_spec(2), wout_spec(3),
            pl.BlockSpec((1, td), lambda j: (0, j)),
            pl.BlockSpec((2 * length, length), lambda j: (0, 0)),
        ],
        out_specs=pl.BlockSpec((HY_ORDER, 2 * length, td), lambda j: (0, 0, j)),
        out_shape=jax.ShapeDtypeStruct((HY_ORDER, 2 * length, D_MODEL), F32),
        compiler_params=_cparams(("arbitrary",)),
        name="hyena_filter",
    )(_hyena_features(length), w1p, b1.reshape(n_hy, 1, HY_FILTER_W), w_hid,
      b_hid.reshape(n_hy, HY_N_SIN - 1, 1, HY_FILTER_W), freq.reshape(n_hy, HY_N_SIN, 1, HY_FILTER_W),
      w_filt_out, w_filt_out, w_filt_out, w_filt_out, _hyena_deltas(), fwd)


def _hy_conv_kernel(pv_ref, p1_ref, p2_ref, wv_ref, w1_ref, w2_ref, k_ref, skip_ref, fwd_ref, inv_ref, o_ref, *, length):
    td = o_ref.shape[-1]
    row = lax.broadcasted_iota(jnp.int32, (length, td), 0)

    def short_conv(p_ref, w_ref):
        p = p_ref[...]
        w = w_ref[...]
        prev = jnp.where(row == 0, 0.0, pltpu.roll(p, 1, 0))
        nxt = jnp.where(row == length - 1, 0.0, pltpu.roll(p, length - 1, 0))
        return prev * w[0:1] + p * w[1:2] + nxt * w[2:3]

    z = short_conv(pv_ref, wv_ref)
    gates = (short_conv(p1_ref, w1_ref), short_conv(p2_ref, w2_ref))
    fwd = fwd_ref[...]
    inv = inv_ref[...]
    for o in range(HY_ORDER):
        zf = jnp.dot(fwd, z.astype(BF16), preferred_element_type=F32)
        zr, zi = zf[:length], zf[length:]
        kr, ki = k_ref[o, :length, :], k_ref[o, length:, :]
        ii = zi * ki
        pr = zr * kr - jnp.where(row == 0, 0.0, ii)
        pi = jnp.where(row == 0, ii, zr * ki + zi * kr)
        conv = jnp.dot(inv, jnp.concatenate([pr, pi], axis=0).astype(BF16), preferred_element_type=F32)
        z = gates[o] * (conv + z * skip_ref[o:o + 1, :])
    o_ref[...] = z.astype(o_ref.dtype)


def hyena_conv(p, row0, n_seq, length, spectrum, w_short, wl, skip, td):
    fwd, inv = _dft_matrices(length)
    nj = D_MODEL // td
    b0 = row0 // length
    p_spec = lambda c: pl.BlockSpec((length, td), lambda b, j: (b0 + b, c * nj + j))
    w_spec = lambda c: pl.BlockSpec((None, HY_SHORT_W, td), lambda b, j: (wl, 0, c * nj + j))
    return pl.pallas_call(
        functools.partial(_hy_conv_kernel, length=length),
        grid=(n_seq, nj),
        in_specs=[
            p_spec(0), p_spec(1), p_spec(2), w_spec(0), w_spec(1), w_spec(2),
            pl.BlockSpec((HY_ORDER, 2 * length, td), lambda b, j: (0, 0, j)),
            pl.BlockSpec((None, HY_ORDER, td), lambda b, j: (wl, 0, j)),
            pl.BlockSpec((2 * length, length), lambda b, j: (0, 0)),
            pl.BlockSpec((length, 2 * length), lambda b, j: (0, 0)),
        ],
        out_specs=pl.BlockSpec((length, td), lambda b, j: (b, j)),
        out_shape=jax.ShapeDtypeStruct((n_seq * length, D_MODEL), BF16),
        compiler_params=_cparams(("arbitrary", "arbitrary")),
        name="hyena_conv",
    )(p, p, p, w_short, w_short, w_short, spectrum, skip, fwd, inv)


def hyena_mixer(h, wl, w_in, w_short, w1, b1, w_hid, b_hid, freq, w_filt_out, skip):
    p = matmul(h, w_in, wl)
    filt = (w1, b1, w_hid, b_hid, freq, w_filt_out)
    z_p = hyena_conv(p, 0, BATCH, SEQ, hyena_filter_spectrum(SEQ, wl, *filt), w_short, wl, skip, td=D_MODEL)
    z_s = hyena_conv(p, T_PROMPT, DEC_BATCH, DEC_SEQ, hyena_filter_spectrum(DEC_SEQ, wl, *filt), w_short, wl, skip, td=512)
    return jnp.concatenate([z_p, z_s], axis=0)


SSD_Q = 128
SSD_R = SSD_HEADS // SSD_GROUPS
SSD_GW = SSD_R * SSD_HEADDIM


def _shift_rows(x, s, row, length):
    if s == 0:
        return x
    rolled = pltpu.roll(x, (-s) % length, 0)
    ok = (row + s >= 0) & (row + s < length)
    return jnp.where(ok, rolled, 0.0)


def _silu(x):
    return x * jax.nn.sigmoid(x)


def _softplus(x):
    return jnp.maximum(x, 0.0) + jnp.log1p(jnp.exp(-jnp.abs(x)))


def _ssd_kernel(*refs, length, has_init, emit_state):
    (z_ref, x_ref, b_ref, c_ref, wx_ref, wb_ref, wc_ref, bx_ref, bb_ref, bc_ref,
     dtc_ref, dtr_ref, dbc_ref, dbr_ref, alc_ref, alr_ref, dsk_ref, nw_ref, exp_ref) = refs[:19]
    pos = 19
    if has_init:
        s0_ref = refs[pos]
        pos += 1
    o_ref = refs[pos]
    pos += 1
    if emit_state:
        so_ref = refs[pos]
        pos += 1
    xc_s, bc_s, cc_s, y_s, st_s = refs[pos:pos + 5]

    hp = functools.partial(jnp.dot, precision=HIGHEST, preferred_element_type=F32)

    def conv_silu(p_ref, w_ref, bias_ref):
        p = p_ref[...]
        w = w_ref[...]
        row = lax.broadcasted_iota(jnp.int32, p.shape, 0)
        acc = bias_ref[...] + _shift_rows(p, -2, row, length) * w[0:1]
        for k in range(1, SSD_CONV_W):
            acc = acc + _shift_rows(p, k - 2, row, length) * w[k:k + 1]
        return _silu(acc)

    xc_s[...] = conv_silu(x_ref, wx_ref, bx_ref)
    bc_s[...] = conv_silu(b_ref, wb_ref, bb_ref)
    cc_s[...] = conv_silu(c_ref, wc_ref, bc_ref)

    expand = exp_ref[...]
    ti = lax.broadcasted_iota(jnp.int32, (SSD_Q, SSD_Q), 0)
    si = lax.broadcasted_iota(jnp.int32, (SSD_Q, SSD_Q), 1)
    lane = lax.broadcasted_iota(jnp.int32, (SSD_Q, 2 * SSD_HEADDIM), 1)
    n_chunks = length // SSD_Q

    for d in range(2):
        if has_init:
            st_s[d] = s0_ref[d]
        else:
            st_s[d] = jnp.zeros((SSD_STATE, SSD_GW), F32)

    def chunk(ci, carry):
        for d in range(2):
            causal = (si <= ti) if d == 0 else (si >= ti)
            tri = causal.astype(F32)
            tri_t = ((ti <= si) if d == 0 else (ti >= si)).astype(F32)
            a_col = -jnp.exp(alc_ref[d])
            a_row = -jnp.exp(alr_ref[d])
            c = ci if d == 0 else n_chunks - 1 - ci
            r0 = pl.multiple_of(c * SSD_Q, SSD_Q)
            rows = pl.ds(r0, SSD_Q)
            dt_c = _softplus(dtc_ref[d, rows, :] + dbc_ref[d])
            dt_r = _softplus(dtr_ref[d, :, rows] + dbr_ref[d])
            acs = hp(tri, dt_c * a_col)
            acs_r = hp(dt_r * a_row, tri_t)
            end = acs[SSD_Q - 1:SSD_Q, :] if d == 0 else acs[0:1, :]
            e_acs = hp(jnp.exp(acs), expand)
            w_in = hp(jnp.exp(end - acs) * dt_c, expand)
            e_end = e_acs[SSD_Q - 1:SSD_Q, :] if d == 0 else e_acs[0:1, :]
            xq = xc_s[rows, :]
            bq = bc_s[rows, :]
            cq = cc_s[rows, :].astype(BF16)
            cb = _nt_dot(cq, bq.astype(BF16))
            st = st_s[d]
            y = e_acs * jnp.dot(cq, st.astype(BF16), preferred_element_type=F32)
            xb = xq.astype(BF16)
            pieces = []
            for pr in range(SSD_R // 2):
                xp = xb[:, pr * 2 * SSD_HEADDIM:(pr + 1) * 2 * SSD_HEADDIM]
                acc = None
                for sub in range(2):
                    r = 2 * pr + sub
                    seg = acs[:, r:r + 1] - acs_r[r:r + 1, :]
                    w = jnp.where(causal, jnp.exp(jnp.where(causal, seg, 0.0)) * cb * dt_r[r:r + 1, :], 0.0)
                    in_head = (lane < SSD_HEADDIM) if sub == 0 else (lane >= SSD_HEADDIM)
                    part = jnp.dot(w.astype(BF16), jnp.where(in_head, xp, jnp.zeros_like(xp)),
                                   preferred_element_type=F32)
                    acc = part if acc is None else acc + part
                pieces.append(acc)
            y_s[d, rows, :] = y + jnp.concatenate(pieces, axis=1)
            upd = lax.dot_general(bq.astype(BF16), (xq * w_in).astype(BF16), (((0,), (0,)), ((), ())),
                                  preferred_element_type=F32)
            st_s[d] = e_end * st + upd
        return carry

    lax.fori_loop(0, n_chunks, chunk, 0)
    if emit_state:
        for d in range(2):
            so_ref[d] = st_s[d]

    y = (xc_s[...] * dsk_ref[...] + y_s[0] + y_s[1]) * _silu(z_ref[...])
    y = y * lax.rsqrt(jnp.mean(y * y, axis=-1, keepdims=True) + EPS) * nw_ref[...]
    o_ref[...] = y.astype(o_ref.dtype)


def ssd_scan(proj, row0, n_seq, length, dt_col, dt_row, wl, conv_w, conv_b, dtb_col, dtb_row, al_col, al_row,
             d_skip_ch, norm_w, s0, emit_state):
    b0 = row0 // length
    gw_blk = SSD_INNER // SSD_GW
    n_blk = SSD_GN // SSD_STATE
    seq_cols = lambda width, base: pl.BlockSpec((length, width), lambda b, g: (b0 + b, base + g))
    par = lambda rows_, width, base: pl.BlockSpec((None, rows_, width), lambda b, g: (wl, 0, base + g))
    in_specs = [
        seq_cols(SSD_GW, 0),
        seq_cols(SSD_GW, gw_blk),
        seq_cols(SSD_STATE, 2 * SSD_INNER // SSD_STATE),
        seq_cols(SSD_STATE, 2 * SSD_INNER // SSD_STATE + n_blk),
        par(SSD_CONV_W, SSD_GW, 0), par(SSD_CONV_W, SSD_STATE, SSD_INNER // SSD_STATE),
        par(SSD_CONV_W, SSD_STATE, SSD_INNER // SSD_STATE + n_blk),
        par(1, SSD_GW, 0), par(1, SSD_STATE, SSD_INNER // SSD_STATE), par(1, SSD_STATE, SSD_INNER // SSD_STATE + n_blk),
        pl.BlockSpec((2, None, length, SSD_R), lambda b, g: (0, g, b0 + b, 0)),
        pl.BlockSpec((2, None, SSD_R, length), lambda b, g: (0, g, 0, b0 + b)),
        pl.BlockSpec((2, None, 1, SSD_R), lambda b, g: (0, g, 0, 0)),
        pl.BlockSpec((2, None, SSD_R, 1), lambda b, g: (0, g, 0, 0)),
        pl.BlockSpec((2, None, 1, SSD_R), lambda b, g: (0, g, 0, 0)),
        pl.BlockSpec((2, None, SSD_R, 1), lambda b, g: (0, g, 0, 0)),
        par(1, SSD_GW, 0), par(1, SSD_GW, 0),
        pl.BlockSpec((SSD_R, SSD_GW), lambda b, g: (0, 0)),
    ]
    expand = jnp.asarray(np.kron(np.eye(SSD_R), np.ones((1, SSD_HEADDIM))), F32)
    args = [proj, proj, proj, proj, conv_w, conv_w, conv_w, conv_b, conv_b, conv_b,
            dt_col, dt_row, dtb_col, dtb_row, al_col, al_row, d_skip_ch, norm_w, expand]
    if s0 is not None:
        in_specs.append(pl.BlockSpec((None, None, 2, SSD_STATE, SSD_GW), lambda b, g: (b, g, 0, 0, 0)))
        args.append(s0)
    out_specs = [pl.BlockSpec((length, SSD_GW), lambda b, g: (b, g))]
    out_shape = [jax.ShapeDtypeStruct((n_seq * length, SSD_INNER), BF16)]
    if emit_state:
        out_specs.append(pl.BlockSpec((None, None, 2, SSD_STATE, SSD_GW), lambda b, g: (b, g, 0, 0, 0)))
        out_shape.append(jax.ShapeDtypeStruct((n_seq, SSD_GROUPS, 2, SSD_STATE, SSD_GW), F32))
    return pl.pallas_call(
        functools.partial(_ssd_kernel, length=length, has_init=s0 is not None, emit_state=emit_state),
        grid=(n_seq, SSD_GROUPS),
        in_specs=in_specs,
        out_specs=out_specs,
        out_shape=out_shape,
        scratch_shapes=[
            pltpu.VMEM((length, SSD_GW), F32), pltpu.VMEM((length, SSD_STATE), F32),
            pltpu.VMEM((length, SSD_STATE), F32), pltpu.VMEM((2, length, SSD_GW), F32),
            pltpu.VMEM((2, SSD_STATE, SSD_GW), F32),
        ],
        compiler_params=_cparams(("arbitrary", "arbitrary")),
        name="ssd_scan",
    )(*args)


def _ssd_state_to_kernel(s):
    b = s.shape[0]
    s = s.reshape(b, 2, SSD_GROUPS, SSD_R, SSD_HEADDIM, SSD_STATE)
    return s.transpose(0, 2, 1, 5, 3, 4).reshape(b, SSD_GROUPS, 2, SSD_STATE, SSD_GW)


def _ssd_state_from_kernel(s):
    b = s.shape[0]
    s = s.reshape(b, SSD_GROUPS, 2, SSD_STATE, SSD_R, SSD_HEADDIM)
    return s.transpose(0, 2, 1, 4, 5, 3).reshape(b, 2, SSD_HEADS, SSD_HEADDIM, SSD_STATE)


def ssd_mixer(h, wl, w_in, conv_w, conv_b, dt_bias, a_log, d_skip, norm_w, state_in):
    proj = matmul(h, w_in, wl, tn=1152)
    t = proj.shape[0]
    dt_raw = proj[:, SSD_INNER + SSD_CONV_DIM:].reshape(t, 2, SSD_GROUPS, SSD_R)
    dt_col = dt_raw.transpose(1, 2, 0, 3)
    dt_row = dt_raw.transpose(1, 2, 3, 0)
    per_group = lambda p: p[wl].reshape(2, SSD_GROUPS, SSD_R)
    dtb, al = per_group(dt_bias), per_group(a_log)
    small = (dtb[:, :, None, :], dtb[:, :, :, None], al[:, :, None, :], al[:, :, :, None])
    n_ssd = conv_b.shape[0]
    common = (wl, conv_w, conv_b.reshape(n_ssd, 1, SSD_CONV_DIM)) + small + (
        jnp.repeat(d_skip, SSD_HEADDIM, axis=-1).reshape(n_ssd, 1, SSD_INNER), norm_w.reshape(n_ssd, 1, SSD_INNER))
    y_p, st = ssd_scan(proj, 0, BATCH, SEQ, dt_col, dt_row, *common, None, True)
    (y_s,) = ssd_scan(proj, T_PROMPT, DEC_BATCH, DEC_SEQ, dt_col, dt_row, *common,
                      _ssd_state_to_kernel(state_in), False)
    return jnp.concatenate([y_p, y_s], axis=0), _ssd_state_from_kernel(st)


HG_HALF = HG_CHUNK // 2
HG_UNROLL = 4


def _block_cumsum(x, row, reverse):
    n = x.shape[0]
    within = row % HG_CHUNK
    step = 1
    while step < HG_CHUNK:
        if reverse:
            shifted = pltpu.roll(x, n - step, 0)
            ok = within < HG_CHUNK - step
        else:
            shifted = pltpu.roll(x, step, 0)
            ok = within >= step
        x = x + jnp.where(ok, shifted, 0.0)
        step *= 2
    return x


def _hg_kernel(*refs, length, has_init, emit_state):
    q_ref, ff_ref, fb_ref, v_ref, gate_ref, lb_ref, gn_ref = refs[:7]
    pos = 7
    if has_init:
        s0_ref = refs[pos]
        pos += 1
    o_ref = refs[pos]
    pos += 1
    if emit_state:
        so_ref = refs[pos]
        pos += 1
    qs_s, g_s, k_s, o_s, st_s = refs[pos:pos + 5]

    row = lax.broadcasted_iota(jnp.int32, (length, HG_DK), 0)
    sub = lax.broadcasted_iota(jnp.int32, (HG_HALF, HG_DK), 0)
    qs_s[...] = _silu(q_ref[...])
    n_blocks = length // HG_CHUNK

    for d, f_ref in enumerate((ff_ref, fb_ref)):
        lb = lb_ref[d]
        f = f_ref[...]
        log_sig = jnp.minimum(f, 0.0) - jnp.log1p(jnp.exp(-jnp.abs(f)))
        a = jnp.log(lb)
        b = jnp.log1p(-lb) + log_sig
        log_g = jnp.maximum(a, b) + jnp.log1p(jnp.exp(-jnp.abs(a - b)))
        g_s[d] = _block_cumsum(log_g, row, reverse=(d == 1))
        k_s[d] = (1.0 - lb) * jax.nn.sigmoid(-f)
        if has_init:
            st_s[d] = s0_ref[d].T
        else:
            st_s[d] = jnp.zeros((HG_DV, HG_DK), F32)

    def block(ci, carry):
        for d in range(2):
            c = ci if d == 0 else n_blocks - 1 - ci
            rows = pl.ds(pl.multiple_of(c * HG_CHUNK, HG_CHUNK), HG_CHUNK)
            gc = g_s[d, rows, :]
            q = qs_s[rows, :]
            k = k_s[d, rows, :]
            v = v_ref[rows, :]
            st = st_s[d]
            g_end = gc[HG_CHUNK - 1:HG_CHUNK, :] if d == 0 else gc[0:1, :]
            o = _nt_dot((q * jnp.exp(gc)).astype(BF16), st.astype(BF16))
            halves = []
            for hh in range(2):
                gh = gc[hh * HG_HALF:(hh + 1) * HG_HALF]
                qh = q[hh * HG_HALF:(hh + 1) * HG_HALF]
                acc = o[hh * HG_HALF:(hh + 1) * HG_HALF]
                for s in range(HG_CHUNK):
                    lo, hi = hh * HG_HALF, (hh + 1) * HG_HALF - 1
                    if (d == 0 and s > hi) or (d == 1 and s < lo):
                        continue
                    w = qh * k[s:s + 1] * jnp.exp(jnp.minimum(gh - gc[s:s + 1], 0.0))
                    if d == 0 and s > lo:
                        w = jnp.where(sub >= s - lo, w, 0.0)
                    if d == 1 and s < hi:
                        w = jnp.where(sub <= s - lo, w, 0.0)
                    acc = acc + jnp.sum(w, axis=-1, keepdims=True) * v[s:s + 1]
                halves.append(acc)
            o_s[d, rows, :] = jnp.concatenate(halves, axis=0)
            k_hat = k * jnp.exp(g_end - gc)
            upd = lax.dot_general(v.astype(BF16), k_hat.astype(BF16), (((0,), (0,)), ((), ())),
                                  preferred_element_type=F32)
            st_s[d] = st * jnp.exp(g_end) + upd
        return carry

    lax.fori_loop(0, n_blocks, block, 0, unroll=HG_UNROLL)
    if emit_state:
        for d in range(2):
            so_ref[d] = st_s[d].T

    o = o_s[0] + o_s[1]
    o = o * lax.rsqrt(jnp.mean(o * o, axis=-1, keepdims=True) + EPS) * gn_ref[...]
    o_ref[...] = (o * _silu(gate_ref[...])).astype(o_ref.dtype)


def hgrn2_scan(proj, row0, n_seq, length, lb, g_norm, wl, s0, emit_state):
    b0 = row0 // length
    col = lambda base: pl.BlockSpec((length, HG_DK), lambda b, h: (b0 + b, base * HG_HEADS + h))
    in_specs = [col(0), col(1), col(2), col(3), col(4),
                pl.BlockSpec((2, 1, HG_DK), lambda b, h: (0, 0, h)),
                pl.BlockSpec((None, 1, HG_DV), lambda b, h: (wl, 0, 0))]
    args = [proj, proj, proj, proj, proj, lb, g_norm]
    state_spec = pl.BlockSpec((None, 2, None, HG_DK, HG_DV), lambda b, h: (b, 0, h, 0, 0))
    if s0 is not None:
        in_specs.append(state_spec)
        args.append(s0)
    out_specs = [pl.BlockSpec((length, HG_DV), lambda b, h: (b, h))]
    out_shape = [jax.ShapeDtypeStruct((n_seq * length, D_MODEL), BF16)]
    if emit_state:
        out_specs.append(state_spec)
        out_shape.append(jax.ShapeDtypeStruct((n_seq, 2, HG_HEADS, HG_DK, HG_DV), F32))
    return pl.pallas_call(
        functools.partial(_hg_kernel, length=length, has_init=s0 is not None, emit_state=emit_state),
        grid=(n_seq, HG_HEADS),
        in_specs=in_specs,
        out_specs=out_specs,
        out_shape=out_shape,
        scratch_shapes=[pltpu.VMEM((length, HG_DK), F32), pltpu.VMEM((2, length, HG_DK), F32),
                        pltpu.VMEM((2, length, HG_DK), F32), pltpu.VMEM((2, length, HG_DV), F32),
                        pltpu.VMEM((2, HG_DV, HG_DK), F32)],
        compiler_params=_cparams(("arbitrary", "arbitrary")),
        name="hgrn2_scan",
    )(*args)


def hgrn2_mixer(h, wl, w_in, lb, g_norm, state_in):
    proj = matmul(h, w_in, wl)
    lb3 = lb.reshape(2, 1, HG_FDIM)
    gn3 = g_norm.reshape(-1, 1, HG_DV)
    o_p, st = hgrn2_scan(proj, 0, BATCH, SEQ, lb3, gn3, wl, None, True)
    (o_s,) = hgrn2_scan(proj, T_PROMPT, DEC_BATCH, DEC_SEQ, lb3, gn3, wl, state_in, False)
    return jnp.concatenate([o_p, o_s], axis=0), st


def _final_norm_kernel(x_ref, g_ref, o_ref):
    x = x_ref[...]
    o_ref[...] = x * lax.rsqrt(jnp.mean(x * x, axis=-1, keepdims=True) + EPS) * g_ref[...]


def final_norm(x, gain):
    t = x.shape[0]
    return pl.pallas_call(
        _final_norm_kernel,
        grid=(t // NORM_TM,),
        in_specs=[pl.BlockSpec((NORM_TM, D_MODEL), lambda i: (i, 0)), pl.BlockSpec((1, D_MODEL), lambda i: (0, 0))],
        out_specs=pl.BlockSpec((NORM_TM, D_MODEL), lambda i: (i, 0)),
        out_shape=jax.ShapeDtypeStruct((t, D_MODEL), F32),
        compiler_params=_cparams(("arbitrary",)),
        name="final_norm",
    )(x, gain.reshape(1, D_MODEL))


def kernel(x_prompt, x_sample, cache_attn_k, cache_attn_v, state_hgrn, state_ssd, c, c_ctx, ada_w, ada_b, norm_mix, norm_ffn, norm_final, hy_w_in, hy_w_short, hy_filt_w1, hy_filt_b1, hy_filt_w_hid, hy_filt_b_hid, hy_filt_freq, hy_filt_w_out, hy_skip, hy_w_out, hg_w_in, hg_lb, hg_norm, hg_w_o, ssd_w_in, ssd_conv_w, ssd_conv_b, ssd_dt_bias, ssd_a_log, ssd_d, ssd_norm, ssd_w_out, at_w_qkv, at_sink, at_w_o, moe_w_router, moe_b_router, moe_w_in, moe_b_in, moe_w_out, moe_b_out):
    x = jnp.concatenate([x_prompt.reshape(T_PROMPT, D_MODEL), x_sample.reshape(T_SAMPLE, D_MODEL)], axis=0)
    cvec = jnp.zeros((N_CVEC, D_MODEL), F32).at[0].set(c_ctx).at[1:1 + DEC_BATCH].set(c)
    mods = ada_mods(cvec, ada_w, ada_b)
    lbs = jax.nn.softmax(hg_lb.astype(F32), axis=1)
    lbs = jnp.cumsum(lbs, axis=1) - lbs[:, :1]
    w_router_t = moe_w_router.transpose(0, 2, 1)
    new_k, new_v, new_hg, new_ssd = [], [], [], []
    for li in range(DEPTH):
        kind, j = li % N_MIXERS, li // N_MIXERS
        h = norm_modulate(x, norm_mix, mods, li, 0)
        if kind == 0:
            y = hyena_mixer(h, j, hy_w_in, hy_w_short, hy_filt_w1, hy_filt_b1, hy_filt_w_hid, hy_filt_b_hid,
                            hy_filt_freq, hy_filt_w_out, hy_skip)
            w_o = hy_w_out
        elif kind == 1:
            y, st = hgrn2_mixer(h, j, hg_w_in, lbs[:, li], hg_norm, state_hgrn[:, j])
            new_hg.append(st)
            w_o = hg_w_o
        elif kind == 2:
            y, st = ssd_mixer(h, j, ssd_w_in, ssd_conv_w, ssd_conv_b, ssd_dt_bias, ssd_a_log, ssd_d, ssd_norm,
                              state_ssd[:, j])
            new_ssd.append(st)
            w_o = ssd_w_out
        else:
            y, kc, vc = attention_mixer(h, at_w_qkv, j, at_sink[j], cache_attn_k[:, j], cache_attn_v[:, j])
            new_k.append(kc)
            new_v.append(vc)
            w_o = at_w_o
        x = matmul_residual(y, w_o, j, x, mods, li, 2)
        x = moe_layer(x, li, mods, norm_ffn, w_router_t, moe_b_router, moe_w_in, moe_b_in, moe_w_out, moe_b_out)
    y = final_norm(x, norm_final)
    y_prompt = y[:T_PROMPT].reshape(BATCH, SEQ, D_MODEL)
    y_sample = y[T_PROMPT:].reshape(DEC_BATCH, DEC_SEQ, D_MODEL)
    return (y_prompt, y_sample, jnp.stack(new_k, axis=1), jnp.stack(new_v, axis=1),
            jnp.stack(new_hg, axis=1), jnp.stack(new_ssd, axis=1))
```

```python
import functools
import math

import numpy as np
import jax
import jax.numpy as jnp
from jax import lax
from jax.experimental import pallas as pl
from jax.experimental.pallas import tpu as pltpu

F32 = jnp.float32
BF16 = jnp.bfloat16
HIGHEST = lax.Precision.HIGHEST

D_MODEL = 2048
BATCH = 32
SEQ = 256
DEPTH = 4
DEC_BATCH = 2
DEC_SEQ = 1024
PAST_LEN = 256
GRID_W = 64
N_MIXERS = 4
N_MOD = 6
EPS = 1e-6

HY_ORDER = 2
HY_SHORT_W = 3
HY_EMB_DIM = 33
HY_BANDS = (HY_EMB_DIM - 1) // 2
HY_FILTER_W = 64
HY_N_SIN = 3
HY_DECAY_TARGET = 1e-2
HY_FAST_DECAY = 0.3
HY_SLOW_DECAY = 1.5

HG_HEADS = D_MODEL // 128
HG_DK = 128
HG_DV = D_MODEL // HG_HEADS
HG_FDIM = HG_HEADS * HG_DK
HG_IN_DIM = 3 * HG_FDIM + 2 * D_MODEL
HG_CHUNK = 16

SSD_INNER = 2 * D_MODEL
SSD_HEADDIM = 64
SSD_HEADS = SSD_INNER // SSD_HEADDIM
SSD_GROUPS = 8
SSD_STATE = 128
SSD_CONV_W = 5
SSD_GN = SSD_GROUPS * SSD_STATE
SSD_CONV_DIM = SSD_INNER + 2 * SSD_GN
SSD_IN_DIM = SSD_INNER + SSD_CONV_DIM + 2 * SSD_HEADS

AT_HEAD_DIM = 64
AT_HEADS = D_MODEL // AT_HEAD_DIM
AT_KV_HEADS = AT_HEADS // 8
AT_REP = AT_HEADS // AT_KV_HEADS
AT_WINDOW = 128
AT_BLOCK = 128
ROPE_THETA = 10000.0

N_EXPERTS = 32
TOP_K = 4
D_EXPERT = D_MODEL
SWIGLU_ALPHA = 1.702
SWIGLU_LIMIT = 7.0

T_PROMPT = BATCH * SEQ
T_SAMPLE = DEC_BATCH * DEC_SEQ
T_ALL = T_PROMPT + T_SAMPLE
N_CVEC = 8

VMEM_LIMIT = 56 * 1024 * 1024


def _cparams(sem):
    return pltpu.CompilerParams(dimension_semantics=sem, vmem_limit_bytes=VMEM_LIMIT)


def _mod_row(tile_idx, tile_rows):
    start = tile_idx * tile_rows
    return jnp.where(start < T_PROMPT, 0, 1 + (start - T_PROMPT) // DEC_SEQ)


ADA_TN = 1024


def _ada_kernel(c_ref, w_ref, b_ref, o_ref):
    c = c_ref[...]
    s = c * jax.nn.sigmoid(c)
    o_ref[...] = jnp.dot(s.astype(BF16), w_ref[...].astype(BF16), preferred_element_type=F32) + b_ref[...]


def ada_mods(cvec, ada_w, ada_b):
    n = N_MOD * D_MODEL
    out = pl.pallas_call(
        _ada_kernel,
        grid=(DEPTH, n // ADA_TN),
        in_specs=[
            pl.BlockSpec((N_CVEC, D_MODEL), lambda l, j: (0, 0)),
            pl.BlockSpec((None, D_MODEL, ADA_TN), lambda l, j: (l, 0, j)),
            pl.BlockSpec((None, 1, ADA_TN), lambda l, j: (l, 0, j)),
        ],
        out_specs=pl.BlockSpec((None, N_CVEC, ADA_TN), lambda l, j: (l, 0, j)),
        out_shape=jax.ShapeDtypeStruct((DEPTH, N_CVEC, n), F32),
        compiler_params=_cparams(("arbitrary", "arbitrary")),
        name="ada_mods",
    )(cvec, ada_w, ada_b.reshape(DEPTH, 1, n))
    return out.reshape(DEPTH, N_CVEC * N_MOD, 1, D_MODEL)


def _mod_spec(li, comp, tile_rows):
    return pl.BlockSpec((None, None, 1, D_MODEL),
                        lambda i, *_: (li, _mod_row(i, tile_rows) * N_MOD + comp, 0, 0))


NORM_TM = 256


def _norm_mod(x, g, sh, sc):
    y = x * lax.rsqrt(jnp.mean(x * x, axis=-1, keepdims=True) + EPS) * g
    return y * (1.0 + sc) + sh


def _norm_kernel(x_ref, g_ref, sh_ref, sc_ref, o_ref):
    o_ref[...] = _norm_mod(x_ref[...], g_ref[...], sh_ref[...], sc_ref[...]).astype(o_ref.dtype)


def norm_modulate(x, gains, mods, li, shift_comp):
    t = x.shape[0]
    return pl.pallas_call(
        _norm_kernel,
        grid=(t // NORM_TM,),
        in_specs=[
            pl.BlockSpec((NORM_TM, D_MODEL), lambda i: (i, 0)),
            pl.BlockSpec((None, 1, D_MODEL), lambda i: (li, 0, 0)),
            _mod_spec(li, shift_comp, NORM_TM),
            _mod_spec(li, shift_comp + 1, NORM_TM),
        ],
        out_specs=pl.BlockSpec((NORM_TM, D_MODEL), lambda i: (i, 0)),
        out_shape=jax.ShapeDtypeStruct((t, D_MODEL), BF16),
        compiler_params=_cparams(("arbitrary",)),
        name="norm_modulate",
    )(x, gains.reshape(DEPTH, 1, D_MODEL), mods, mods)


def _norm_router_kernel(x_ref, g_ref, sh_ref, sc_ref, wr_ref, br_ref, o_ref, ti_ref, tg_ref):
    h = _norm_mod(x_ref[...], g_ref[...], sh_ref[...], sc_ref[...])
    o_ref[...] = _pack_bf16_pairs(h)
    logits = lax.dot_general(wr_ref[...], h, (((1,), (1,)), ((), ())),
                             precision=HIGHEST, preferred_element_type=F32) + br_ref[...]
    eidx = lax.broadcasted_iota(jnp.int32, logits.shape, 0)
    vals = logits
    top_v = []
    for k in range(TOP_K):
        m = jnp.max(vals, axis=0, keepdims=True)
        sel = jnp.min(jnp.where(vals == m, eidx, N_EXPERTS), axis=0, keepdims=True)
        top_v.append(m)
        ti_ref[k:k + 1, :] = sel
        vals = jnp.where(eidx == sel, -jnp.inf, vals)
    ex = [jnp.exp(v - top_v[0]) for v in top_v]
    den = ex[0] + ex[1] + ex[2] + ex[3]
    for k in range(TOP_K):
        tg_ref[k:k + 1, :] = ex[k] / den


def norm_router(x, gains, mods, li, w_router_t, b_router):
    t = x.shape[0]
    return pl.pallas_call(
        _norm_router_kernel,
        grid=(t // NORM_TM,),
        in_specs=[
            pl.BlockSpec((NORM_TM, D_MODEL), lambda i: (i, 0)),
            pl.BlockSpec((None, 1, D_MODEL), lambda i: (li, 0, 0)),
            _mod_spec(li, 3, NORM_TM),
            _mod_spec(li, 4, NORM_TM),
            pl.BlockSpec((None, N_EXPERTS, D_MODEL), lambda i: (li, 0, 0)),
            pl.BlockSpec((None, N_EXPERTS, 1), lambda i: (li, 0, 0)),
        ],
        out_specs=[
            pl.BlockSpec((NORM_TM, D_MODEL // 2), lambda i: (i, 0)),
            pl.BlockSpec((TOP_K, NORM_TM), lambda i: (0, i)),
            pl.BlockSpec((TOP_K, NORM_TM), lambda i: (0, i)),
        ],
        out_shape=[
            jax.ShapeDtypeStruct((t, D_MODEL // 2), jnp.uint32),
            jax.ShapeDtypeStruct((TOP_K, t), jnp.int32),
            jax.ShapeDtypeStruct((TOP_K, t), F32),
        ],
        compiler_params=_cparams(("arbitrary",)),
        name="norm_router",
    )(x, gains.reshape(DEPTH, 1, D_MODEL), mods, mods, w_router_t, b_router.reshape(DEPTH, N_EXPERTS, 1))


MM_TM = 1024


def _mm_kernel(x_ref, w_ref, o_ref, wb_ref):
    @pl.when(pl.program_id(1) == 0)
    def _():
        wb_ref[...] = w_ref[...].astype(BF16)

    o_ref[...] = jnp.dot(x_ref[...], wb_ref[...], preferred_element_type=F32).astype(o_ref.dtype)


def _mm_res_kernel(x_ref, w_ref, r_ref, g_ref, o_ref, wb_ref):
    @pl.when(pl.program_id(1) == 0)
    def _():
        wb_ref[...] = w_ref[...].astype(BF16)

    o_ref[...] = r_ref[...] + g_ref[...] * jnp.dot(x_ref[...], wb_ref[...], preferred_element_type=F32)


def _w_spec(w, wl, k, tn):
    if w.ndim == 3:
        return pl.BlockSpec((None, k, tn), lambda j, i: (wl, 0, j))
    return pl.BlockSpec((k, tn), lambda j, i: (0, j))


def matmul(x, w, wl=0, out_dtype=F32, tn=1024):
    t, k = x.shape
    n = w.shape[-1]
    assert n % tn == 0 and t % MM_TM == 0
    return pl.pallas_call(
        _mm_kernel,
        grid=(n // tn, t // MM_TM),
        in_specs=[pl.BlockSpec((MM_TM, k), lambda j, i: (i, 0)), _w_spec(w, wl, k, tn)],
        out_specs=pl.BlockSpec((MM_TM, tn), lambda j, i: (i, j)),
        out_shape=jax.ShapeDtypeStruct((t, n), out_dtype),
        scratch_shapes=[pltpu.VMEM((k, tn), BF16)],
        compiler_params=_cparams(("arbitrary", "arbitrary")),
        name="matmul",
    )(x, w)


def matmul_residual(x, w, wl, res, mods, li, gate_comp, tn=512):
    t, k = x.shape
    n = w.shape[-1]
    assert n % tn == 0 and t % MM_TM == 0
    gate_spec = pl.BlockSpec((None, None, 1, tn),
                             lambda j, i: (li, _mod_row(i, MM_TM) * N_MOD + gate_comp, 0, j))
    return pl.pallas_call(
        _mm_res_kernel,
        grid=(n // tn, t // MM_TM),
        in_specs=[pl.BlockSpec((MM_TM, k), lambda j, i: (i, 0)), _w_spec(w, wl, k, tn),
                  pl.BlockSpec((MM_TM, tn), lambda j, i: (i, j)), gate_spec],
        out_specs=pl.BlockSpec((MM_TM, tn), lambda j, i: (i, j)),
        out_shape=jax.ShapeDtypeStruct((t, n), F32),
        scratch_shapes=[pltpu.VMEM((k, tn), BF16)],
        compiler_params=_cparams(("arbitrary", "arbitrary")),
        name="matmul_residual",
    )(x, w, res, mods)


MOE_BLK = 128
MOE_SB_BLKS = 12
MOE_M_BLKS = (6, 8, 9, 10, 11, 12)
MOE_RB = MOE_BLK * MOE_SB_BLKS
MOE_TH = 256
MOE_NJ = D_EXPERT // MOE_TH
D_PACK = D_MODEL // 2


def _moe_nblk(t):
    return pl.cdiv(t * TOP_K, MOE_BLK) + N_EXPERTS


def _moe_nsb(t):
    return pl.cdiv(t * TOP_K, MOE_RB) + N_EXPERTS


def moe_dispatch(top_i):
    t = top_i.shape[1]
    moe_rows, moe_nsb = _moe_nblk(t) * MOE_BLK, _moe_nsb(t)
    flat_e = top_i.T.reshape(-1)
    onehot = (flat_e[:, None] == jnp.arange(N_EXPERTS, dtype=jnp.int32)[None, :]).astype(jnp.int32)
    csum = jnp.cumsum(onehot, axis=0)
    rank = jnp.take_along_axis(csum, flat_e[:, None], axis=1)[:, 0] - 1
    counts = csum[-1]
    nblk = (counts + MOE_BLK - 1) // MOE_BLK
    blk_end = jnp.cumsum(nblk)
    blk_start = blk_end - nblk
    pos = blk_start[flat_e] * MOE_BLK + rank
    tok_buf = jnp.zeros((moe_rows,), jnp.int32).at[pos].set(jnp.arange(t * TOP_K, dtype=jnp.int32) // TOP_K)
    n_blk = blk_end[-1]
    nsb_e = (nblk + MOE_SB_BLKS - 1) // MOE_SB_BLKS
    sb_end = jnp.cumsum(nsb_e)
    sb_start = sb_end - nsb_e
    n_sb = sb_end[-1]
    sb = jnp.arange(moe_nsb, dtype=jnp.int32)
    e_of = jnp.minimum(jnp.searchsorted(sb_end, sb, side='right'), N_EXPERTS - 1).astype(jnp.int32)
    local = sb - sb_start[e_of]
    valid = sb < n_sb
    e_last = e_of[jnp.maximum(n_sb - 1, 0)]
    sb_e = jnp.where(valid, e_of, e_last).astype(jnp.int32)
    sb_blk0 = jnp.where(valid, blk_start[e_of] + local * MOE_SB_BLKS, 0).astype(jnp.int32)
    sb_nb = jnp.where(valid, jnp.clip(nblk[e_of] - local * MOE_SB_BLKS, 0, MOE_SB_BLKS), 0).astype(jnp.int32)
    return dict(pos=pos.astype(jnp.int32), tok_blk=tok_buf.reshape(moe_rows // MOE_BLK, 1, MOE_BLK),
                sb_e=sb_e, sb_blk0=sb_blk0, sb_nb=sb_nb, n_sb=n_sb.reshape(1).astype(jnp.int32))


def _pack_bf16_pairs(h):
    half = h.shape[-1] // 2
    lo = lax.bitcast_convert_type(h[:, :half].astype(BF16).astype(F32), jnp.uint32)
    hi = lax.bitcast_convert_type(h[:, half:].astype(BF16).astype(F32), jnp.uint32)
    return (hi & jnp.uint32(0xFFFF0000)) | (lo >> 16)


def _unpack_bf16_pairs(p):
    lo = lax.bitcast_convert_type(p << 16, F32).astype(BF16)
    hi = lax.bitcast_convert_type(p & jnp.uint32(0xFFFF0000), F32).astype(BF16)
    return lo, hi


MOE_DMA_UNROLL = 8


def _moe_kernel(sbe_ref, sbb_ref, sbn_ref, nsb_ref, *refs):
    tok_refs = refs[:MOE_SB_BLKS]
    (hp_ref, wg_ref, wu_ref, bg_ref, bu_ref, wo_ref, bo_ref,
     ys_ref,
     stage, xb_buf, y_acc, sem_in, sem_out) = refs[MOE_SB_BLKS:]
    sb = pl.program_id(0)
    j = pl.program_id(1)
    nb = sbn_ref[sb]
    row0 = sbb_ref[sb] * MOE_BLK

    def blk(rb):
        return pl.ds(rb * MOE_BLK, MOE_BLK)

    def for_block_rows(rb, fn):
        def body(i, c):
            fn(pltpu.make_async_copy(hp_ref.at[pl.ds(tok_refs[rb][0, i], 1)],
                                     stage.at[rb % 2, pl.ds(i, 1)], sem_in.at[rb % 2]))
            return c
        lax.fori_loop(0, MOE_BLK, body, 0, unroll=MOE_DMA_UNROLL)

    def unpack(rb):
        lo, hi = _unpack_bf16_pairs(stage[rb % 2])
        xb_buf[blk(rb), :D_PACK] = lo
        xb_buf[blk(rb), D_PACK:] = hi

    def out_copy(rb, first_row):
        rows = pl.ds(pl.multiple_of(rb * MOE_BLK, MOE_BLK), MOE_BLK)
        return pltpu.make_async_copy(y_acc.at[rows], ys_ref.at[pl.ds(first_row + rb * MOE_BLK, MOE_BLK)], sem_out)

    def for_out_copies(n, first_row, fn):
        def body(rb, c):
            fn(out_copy(rb, first_row))
            return c
        lax.fori_loop(0, n, body, 0)

    @pl.when(jnp.logical_and(sb > 0, j == 0))
    def _():
        prev = jnp.maximum(sb - 1, 0)
        for_out_copies(sbn_ref[prev], sbb_ref[prev] * MOE_BLK, lambda cp: cp.wait())

    @pl.when(jnp.logical_and(nb > 0, j == 0))
    def _():
        for rb in range(MOE_SB_BLKS + 1):
            if rb < MOE_SB_BLKS:
                pl.when(rb < nb)(functools.partial(for_block_rows, rb, lambda cp: cp.start()))
            if rb >= 1:
                @pl.when(rb - 1 < nb)
                def _(rb=rb):
                    for_block_rows(rb - 1, lambda cp: cp.wait())
                    unpack(rb - 1)
        for rb in range(1, MOE_M_BLKS[0]):
            @pl.when(nb <= rb)
            def _(rb=rb):
                xb_buf[blk(rb), :] = jnp.zeros((MOE_BLK, D_MODEL), BF16)
        y_acc[...] = jnp.broadcast_to(bo_ref[...], y_acc.shape)

    def ffn(m_blks):
        m = m_blks * MOE_BLK
        x = xb_buf[:m, :]
        hg = jnp.dot(x, wg_ref[...].astype(BF16), preferred_element_type=F32) + bg_ref[...]
        hu = jnp.dot(x, wu_ref[...].astype(BF16), preferred_element_type=F32) + bu_ref[...]
        g = jnp.minimum(hg, SWIGLU_LIMIT)
        u = jnp.clip(hu, -SWIGLU_LIMIT, SWIGLU_LIMIT)
        a = (u + 1.0) * g * jax.nn.sigmoid(SWIGLU_ALPHA * g)
        y_acc[:m, :] += jnp.dot(a.astype(BF16), wo_ref[...].astype(BF16), preferred_element_type=F32)

    for idx, m_blks in enumerate(MOE_M_BLKS):
        below = MOE_M_BLKS[idx - 1] if idx else 0
        pl.when(jnp.logical_and(nb > below, nb <= m_blks))(functools.partial(ffn, m_blks))

    @pl.when(j == MOE_NJ - 1)
    def _():
        for_out_copies(nb, row0, lambda cp: cp.start())

        @pl.when(sb == pl.num_programs(0) - 1)
        def _():
            for_out_copies(nb, row0, lambda cp: cp.wait())


def moe_experts(hp, tables, w_in, b_in, w_out, b_out, li):
    n_rows = _moe_nblk(hp.shape[0]) * MOE_BLK

    def live_j(sb, j, nsb):
        return jnp.where(sb < nsb[0], j, MOE_NJ - 1)

    n_blk_max = tables["tok_blk"].shape[0]
    tok_spec = lambda rb: pl.BlockSpec(
        (None, 1, MOE_BLK), lambda sb, j, sbe, sbb, sbn, nsb: (jnp.minimum(sbb[sb] + rb, n_blk_max - 1), 0, 0),
        memory_space=pltpu.SMEM)

    w_in_spec = lambda half: pl.BlockSpec(
        (None, None, D_MODEL, MOE_TH),
        lambda sb, j, sbe, sbb, sbn, nsb: (li, sbe[sb], 0, half * MOE_NJ + live_j(sb, j, nsb)))
    b_in_spec = lambda half: pl.BlockSpec(
        (None, None, 1, MOE_TH),
        lambda sb, j, sbe, sbb, sbn, nsb: (li, sbe[sb], 0, half * MOE_NJ + live_j(sb, j, nsb)))
    return pl.pallas_call(
        _moe_kernel,
        grid_spec=pltpu.PrefetchScalarGridSpec(
            num_scalar_prefetch=4,
            grid=(tables["sb_e"].shape[0], MOE_NJ),
            in_specs=[tok_spec(rb) for rb in range(MOE_SB_BLKS)] + [
                pl.BlockSpec(memory_space=pl.ANY),
                w_in_spec(0), w_in_spec(1), b_in_spec(0), b_in_spec(1),
                pl.BlockSpec((None, None, MOE_TH, D_MODEL),
                             lambda sb, j, sbe, sbb, sbn, nsb: (li, sbe[sb], live_j(sb, j, nsb), 0)),
                pl.BlockSpec((None, None, 1, D_MODEL), lambda sb, j, sbe, sbb, sbn, nsb: (li, sbe[sb], 0, 0)),
            ],
            out_specs=pl.BlockSpec(memory_space=pl.ANY),
            scratch_shapes=[
                pltpu.VMEM((2, MOE_BLK, D_PACK), jnp.uint32),
                pltpu.VMEM((MOE_RB, D_MODEL), BF16),
                pltpu.VMEM((MOE_RB, D_MODEL), F32),
                pltpu.SemaphoreType.DMA((2,)),
                pltpu.SemaphoreType.DMA(()),
            ],
        ),
        out_shape=jax.ShapeDtypeStruct((n_rows, D_MODEL), F32),
        compiler_params=_cparams(("arbitrary", "arbitrary")),
        name="moe_experts",
    )(tables["sb_e"], tables["sb_blk0"], tables["sb_nb"], tables["n_sb"],
      *([tables["tok_blk"]] * MOE_SB_BLKS), hp, w_in, w_in, b_in.reshape(-1, N_EXPERTS, 1, 2 * D_EXPERT), b_in.reshape(-1, N_EXPERTS, 1, 2 * D_EXPERT),
      w_out, b_out.reshape(-1, N_EXPERTS, 1, D_MODEL))


CMB_TM = 128


def _combine_kernel(pos_ref, posn_ref, ys_ref, x_ref, tg_ref, mg_ref, o_ref, buf, sems):
    i = pl.program_id(0)
    nt = pl.num_programs(0)
    slot = i % 2

    def for_rows(idx_ref, dst_slot, fn):
        for k in range(TOP_K):
            def body(r, c, k=k):
                fn(pltpu.make_async_copy(ys_ref.at[pl.ds(idx_ref[0, k * CMB_TM + r], 1)],
                                         buf.at[dst_slot, k, pl.ds(r, 1)], sems.at[dst_slot]))
                return c
            lax.fori_loop(0, CMB_TM, body, 0, unroll=MOE_DMA_UNROLL)

    @pl.when(i == 0)
    def _():
        for_rows(pos_ref, 0, lambda cp: cp.start())

    @pl.when(i + 1 < nt)
    def _():
        for_rows(posn_ref, 1 - slot, lambda cp: cp.start())

    for_rows(pos_ref, slot, lambda cp: cp.wait())
    tg = tg_ref[...]
    y = tg[:, 0:1] * buf[slot, 0]
    for k in range(1, TOP_K):
        y = y + tg[:, k:k + 1] * buf[slot, k]
    o_ref[...] = x_ref[...] + mg_ref[...] * y


def moe_combine(ys, pos, gates_t, x, mods, li):
    t = x.shape[0]
    nt = t // CMB_TM
    pos_tiles = pos.reshape(nt, CMB_TM, TOP_K).transpose(0, 2, 1).reshape(nt, 1, TOP_K * CMB_TM)
    pos_spec = lambda shift: pl.BlockSpec((None, 1, TOP_K * CMB_TM),
                                          lambda i: (jnp.minimum(i + shift, nt - 1), 0, 0), memory_space=pltpu.SMEM)
    return pl.pallas_call(
        _combine_kernel,
        grid=(nt,),
        in_specs=[
            pos_spec(0), pos_spec(1),
            pl.BlockSpec(memory_space=pl.ANY),
            pl.BlockSpec((CMB_TM, D_MODEL), lambda i: (i, 0)),
            pl.BlockSpec((CMB_TM, TOP_K), lambda i: (i, 0)),
            _mod_spec(li, 5, CMB_TM),
        ],
        out_specs=pl.BlockSpec((CMB_TM, D_MODEL), lambda i: (i, 0)),
        out_shape=jax.ShapeDtypeStruct((t, D_MODEL), F32),
        scratch_shapes=[pltpu.VMEM((2, TOP_K, CMB_TM, D_MODEL), F32), pltpu.SemaphoreType.DMA((2,))],
        compiler_params=_cparams(("arbitrary",)),
        name="moe_combine",
    )(pos_tiles, pos_tiles, ys, x, gates_t, mods)


def moe_layer(x, li, mods, norm_ffn, w_router_t, b_router, w_in, b_in, w_out, b_out):
    hp, top_i, top_g = norm_router(x, norm_ffn, mods, li, w_router_t, b_router)
    tables = moe_dispatch(top_i)
    ys = moe_experts(hp, tables, w_in, b_in, w_out, b_out, li)
    return moe_combine(ys, tables["pos"], top_g.T, x, mods, li)


AT_NH = AT_HEADS + 2 * AT_KV_HEADS
AT_SCALE = AT_HEAD_DIM ** -0.5


def _rope_tables(length):
    pos = np.arange(length)
    half, quarter = AT_HEAD_DIM // 2, AT_HEAD_DIM // 4
    inv = ROPE_THETA ** (-np.arange(0, half, 2, dtype=np.float64) / half)
    j = np.arange(AT_HEAD_DIM)
    p = np.where(j[None, :] < half, (pos // GRID_W)[:, None], (pos % GRID_W)[:, None]).astype(np.float64)
    ang = p * inv[(j % half) % quarter][None, :]
    first = (j % half) < quarter
    sin_signed = np.where(first[None, :], -np.sin(ang), np.sin(ang))
    partner = np.where(first, j + quarter, j - quarter)
    perm = np.zeros((AT_HEAD_DIM, AT_HEAD_DIM), np.float32)
    perm[partner, j] = 1.0
    return jnp.asarray(np.cos(ang), F32), jnp.asarray(sin_signed, F32), jnp.asarray(perm)


def _rope_kernel(x_ref, cos_ref, sin_ref, perm_ref, o_ref):
    x = x_ref[...]
    xp = jnp.dot(x, perm_ref[...], precision=HIGHEST, preferred_element_type=F32)
    o_ref[...] = x * cos_ref[...] + xp * sin_ref[...]


def rope_heads(xh, n_heads):
    b, _, length, dh = xh.shape
    cos, sin, perm = _rope_tables(length)
    return pl.pallas_call(
        _rope_kernel,
        grid=(b, n_heads),
        in_specs=[
            pl.BlockSpec((None, None, length, dh), lambda i, h: (i, h, 0, 0)),
            pl.BlockSpec((length, dh), lambda i, h: (0, 0)),
            pl.BlockSpec((length, dh), lambda i, h: (0, 0)),
            pl.BlockSpec((dh, dh), lambda i, h: (0, 0)),
        ],
        out_specs=pl.BlockSpec((None, None, length, dh), lambda i, h: (i, h, 0, 0)),
        out_shape=jax.ShapeDtypeStruct((b, n_heads, length, dh), F32),
        compiler_params=_cparams(("arbitrary", "arbitrary")),
        name="rope_heads",
    )(xh, cos, sin, perm)


def _nt_dot(a, b):
    return lax.dot_general(a, b, (((1,), (1,)), ((), ())), preferred_element_type=F32)


def _attn_ctx_kernel(sink_ref, q_ref, k_ref, v_ref, o_ref):
    g = pl.program_id(1)
    k = k_ref[...].astype(BF16)
    v = v_ref[...].astype(BF16)
    for r in range(AT_REP):
        q = (q_ref[r] * AT_SCALE).astype(BF16)
        s = _nt_dot(q, k)
        sk = sink_ref[g * AT_REP + r]
        m = jnp.maximum(jnp.max(s, axis=-1, keepdims=True), sk)
        p = jnp.exp(s - m)
        den = jnp.sum(p, axis=-1, keepdims=True) + jnp.exp(sk - m)
        o = jnp.dot(p.astype(BF16), v, preferred_element_type=F32) / den
        o_ref[r] = o.astype(o_ref.dtype)


def attn_context(qkvh, sink):
    b, _, length, dh = qkvh.shape
    return pl.pallas_call(
        _attn_ctx_kernel,
        grid=(b, AT_KV_HEADS),
        in_specs=[
            pl.BlockSpec(memory_space=pltpu.SMEM),
            pl.BlockSpec((None, AT_REP, length, dh), lambda i, g: (i, g, 0, 0)),
            pl.BlockSpec((None, None, length, dh), lambda i, g: (i, AT_HEADS + g, 0, 0)),
            pl.BlockSpec((None, None, length, dh), lambda i, g: (i, AT_HEADS + AT_KV_HEADS + g, 0, 0)),
        ],
        out_specs=pl.BlockSpec((None, AT_REP, length, dh), lambda i, g: (i, g, 0, 0)),
        out_shape=jax.ShapeDtypeStruct((b, AT_HEADS, length, dh), BF16),
        compiler_params=_cparams(("arbitrary", "arbitrary")),
        name="attn_context",
    )(sink, qkvh, qkvh, qkvh)


def _attn_lat_kernel(sink_ref, q_ref, k0_ref, k1_ref, k2_ref, v0_ref, v1_ref, v2_ref, kc_ref, vc_ref, o_ref, *, length):
    g = pl.program_id(1)
    i = pl.program_id(2)
    kb = jnp.concatenate([k0_ref[...], k1_ref[...], k2_ref[...]], axis=0).astype(BF16)
    vb = jnp.concatenate([v0_ref[...], v1_ref[...], v2_ref[...]], axis=0).astype(BF16)
    kc = kc_ref[...].astype(BF16)
    vc = vc_ref[...].astype(BF16)
    qpos = i * AT_BLOCK + lax.broadcasted_iota(jnp.int32, (AT_BLOCK, 3 * AT_BLOCK), 0)
    kpos = (i - 1) * AT_BLOCK + lax.broadcasted_iota(jnp.int32, (AT_BLOCK, 3 * AT_BLOCK), 1)
    allowed = (jnp.abs(qpos - kpos) <= AT_WINDOW) & (kpos >= 0) & (kpos < length)
    for r in range(AT_REP):
        q = (q_ref[r] * AT_SCALE).astype(BF16)
        s1 = jnp.where(allowed, _nt_dot(q, kb), -jnp.inf)
        s2 = _nt_dot(q, kc)
        sk = sink_ref[g * AT_REP + r]
        m = jnp.maximum(jnp.maximum(jnp.max(s1, axis=-1, keepdims=True), jnp.max(s2, axis=-1, keepdims=True)), sk)
        p1 = jnp.exp(s1 - m)
        p2 = jnp.exp(s2 - m)
        den = jnp.sum(p1, axis=-1, keepdims=True) + jnp.sum(p2, axis=-1, keepdims=True) + jnp.exp(sk - m)
        o = (jnp.dot(p1.astype(BF16), vb, preferred_element_type=F32)
             + jnp.dot(p2.astype(BF16), vc, preferred_element_type=F32)) / den
        o_ref[r] = o.astype(o_ref.dtype)


def attn_latent(qk_rope, qkvh, k_ctx, v_ctx, sink):
    b, _, length, dh = qkvh.shape
    nblk = length // AT_BLOCK
    lc = k_ctx.shape[2]
    band = lambda head0, shift: pl.BlockSpec(
        (None, None, AT_BLOCK, dh), lambda bi, g, i: (bi, head0 + g, jnp.clip(i + shift, 0, nblk - 1), 0))
    ctx = pl.BlockSpec((None, None, lc, dh), lambda bi, g, i: (bi, g, 0, 0))
    return pl.pallas_call(
        functools.partial(_attn_lat_kernel, length=length),
        grid=(b, AT_KV_HEADS, nblk),
        in_specs=[
            pl.BlockSpec(memory_space=pltpu.SMEM),
            pl.BlockSpec((None, AT_REP, AT_BLOCK, dh), lambda bi, g, i: (bi, g, i, 0)),
            band(AT_HEADS, -1), band(AT_HEADS, 0), band(AT_HEADS, 1),
            band(AT_HEADS + AT_KV_HEADS, -1), band(AT_HEADS + AT_KV_HEADS, 0), band(AT_HEADS + AT_KV_HEADS, 1),
            ctx, ctx,
        ],
        out_specs=pl.BlockSpec((None, AT_REP, AT_BLOCK, dh), lambda bi, g, i: (bi, g, i, 0)),
        out_shape=jax.ShapeDtypeStruct((b, AT_HEADS, length, dh), BF16),
        compiler_params=_cparams(("arbitrary", "arbitrary", "arbitrary")),
        name="attn_latent",
    )(sink, qk_rope, qk_rope, qk_rope, qk_rope, qkvh, qkvh, qkvh, k_ctx, v_ctx)


def attention_mixer(h, w_qkv, wl, sink, cache_k, cache_v):
    qkv = matmul(h, w_qkv, wl, tn=512)
    nkv = AT_KV_HEADS * AT_HEAD_DIM
    new_k = qkv[:T_PROMPT, D_MODEL:D_MODEL + nkv].reshape(BATCH, SEQ, AT_KV_HEADS, AT_HEAD_DIM)
    new_v = qkv[:T_PROMPT, D_MODEL + nkv:].reshape(BATCH, SEQ, AT_KV_HEADS, AT_HEAD_DIM)
    qkv_p = qkv[:T_PROMPT].reshape(BATCH, SEQ, AT_NH, AT_HEAD_DIM).transpose(0, 2, 1, 3)
    qkv_s = qkv[T_PROMPT:].reshape(DEC_BATCH, DEC_SEQ, AT_NH, AT_HEAD_DIM).transpose(0, 2, 1, 3)
    o_p = attn_context(qkv_p, sink)
    qk_rope = rope_heads(qkv_s, AT_HEADS + AT_KV_HEADS)
    o_s = attn_latent(qk_rope, qkv_s, cache_k.transpose(0, 2, 1, 3), cache_v.transpose(0, 2, 1, 3), sink)
    o = jnp.concatenate([o_p.transpose(0, 2, 1, 3).reshape(T_PROMPT, D_MODEL),
                         o_s.transpose(0, 2, 1, 3).reshape(T_SAMPLE, D_MODEL)], axis=0)
    return o, new_k, new_v


HY_FEAT_PAD = 128


def _dft_matrices(length):
    n = np.arange(length, dtype=np.float64)
    ang = np.pi * np.outer(n, n) / length
    alt = np.where(n % 2 == 0, 1.0, -1.0)
    fwd_im = -np.sin(ang)
    fwd_im[0, :] = alt
    fwd = np.concatenate([np.cos(ang), fwd_im], axis=0)
    wk = np.where(n == 0, 1.0, 2.0)[None, :] / (2.0 * length)
    inv_im = -wk * np.sin(ang)
    inv_im[:, 0] = alt / (2.0 * length)
    inv = np.concatenate([wk * np.cos(ang), inv_im], axis=1)
    return jnp.asarray(fwd, F32).astype(BF16), jnp.asarray(inv, F32).astype(BF16)


def _hyena_features(length):
    t = np.linspace(0.0, 1.0, length)[:, None]
    om = 2.0 * np.pi * np.arange(length)[:, None] / length
    f = np.linspace(1e-4, HY_BANDS - 1, HY_BANDS)[None, :]
    z = np.concatenate([t, np.cos(f * om), -np.sin(f * om)], axis=-1)
    return jnp.asarray(np.pad(z, ((0, 0), (0, HY_FEAT_PAD - HY_EMB_DIM))), F32)


def _hyena_deltas():
    max_decay = math.log(HY_DECAY_TARGET) / HY_FAST_DECAY
    min_decay = math.log(HY_DECAY_TARGET) / HY_SLOW_DECAY
    return jnp.asarray(np.abs(np.linspace(min_decay, max_decay, D_MODEL)), F32).reshape(1, D_MODEL)


def _hy_filter_kernel(z_ref, w1_ref, b1_ref, wh_ref, bh_ref, fr_ref, wo00, wo01, wo10, wo11, dl_ref, fwd_ref, k_ref,
                      *, length):
    hp = functools.partial(jnp.dot, precision=HIGHEST, preferred_element_type=F32)
    h = jnp.sin(fr_ref[0] * (hp(z_ref[...], w1_ref[...]) + b1_ref[...]))
    for s in range(HY_N_SIN - 1):
        h = jnp.sin(fr_ref[s + 1] * (hp(h, wh_ref[s]) + bh_ref[s]))
    td = dl_ref.shape[-1]
    row = lax.broadcasted_iota(jnp.int32, (length, td), 0)
    t = row.astype(F32) * (1.0 / (length - 1))
    decay = jnp.exp(-t * dl_ref[...])
    fwd = fwd_ref[...]
    for o, (wf, wb) in enumerate(((wo00, wo01), (wo10, wo11))):
        hf = hp(h, wf[...]) * decay
        hb = jnp.where(row == 0, 0.0, hp(h, wb[...]) * decay)
        ks = jnp.dot(fwd, (hf + hb).astype(BF16), preferred_element_type=F32)
        kd = jnp.dot(fwd[length:], (hf - hb).astype(BF16), preferred_element_type=F32)
        k_ref[o, :length, :] = ks[:length]
        k_ref[o, length:, :] = jnp.where(row == 0, ks[length:length + 1], kd)


def hyena_filter_spectrum(length, wl, w1, b1, w_hid, b_hid, freq, w_filt_out, td=512):
    fwd, _ = _dft_matrices(length)
    nj = D_MODEL // td
    w1p = jnp.pad(w1, ((0, 0), (0, HY_FEAT_PAD - HY_EMB_DIM), (0, 0)))
    n_hy = w1.shape[0]
    wout_spec = lambda c: pl.BlockSpec((None, HY_FILTER_W, td), lambda j: (wl, 0, c * nj + j))
    whole = lambda shape: pl.BlockSpec((None,) + shape, lambda j: (wl,) + (0,) * len(shape))
    return pl.pallas_call(
        functools.partial(_hy_filter_kernel, length=length),
        grid=(nj,),
        in_specs=[
            pl.BlockSpec((length, HY_FEAT_PAD), lambda j: (0, 0)),
            whole((HY_FEAT_PAD, HY_FILTER_W)), whole((1, HY_FILTER_W)),
            whole((HY_N_SIN - 1, HY_FILTER_W, HY_FILTER_W)), whole((HY_N_SIN - 1, 1, HY_FILTER_W)),
            whole((HY_N_SIN, 1, HY_FILTER_W)),
            wout_spec(0), wout_spec(1), wout_spec(2), wout_spec(3),
            pl.BlockSpec((1, td), lambda j: (0, j)),
            pl.BlockSpec((2 * length, length), lambda j: (0, 0)),
        ],
        out_specs=pl.BlockSpec((HY_ORDER, 2 * length, td), lambda j: (0, 0, j)),
        out_shape=jax.ShapeDtypeStruct((HY_ORDER, 2 * length, D_MODEL), F32),
        compiler_params=_cparams(("arbitrary",)),
        name="hyena_filter",
    )(_hyena_features(length), w1p, b1.reshape(n_hy, 1, HY_FILTER_W), w_hid,
      b_hid.reshape(n_hy, HY_N_SIN - 1, 1, HY_FILTER_W), freq.reshape(n_hy, HY_N_SIN, 1, HY_FILTER_W),
      w_filt_out, w_filt_out, w_filt_out, w_filt_out, _hyena_deltas(), fwd)


def _hy_conv_kernel(pv_ref, p1_ref, p2_ref, wv_ref, w1_ref, w2_ref, k_ref, skip_ref, fwd_ref, inv_ref, o_ref, *, length):
    td = o_ref.shape[-1]
    row = lax.broadcasted_iota(jnp.int32, (length, td), 0)

    def short_conv(p_ref, w_ref):
        p = p_ref[...]
        w = w_ref[...]
        prev = jnp.where(row == 0, 0.0, pltpu.roll(p, 1, 0))
        nxt = jnp.where(row == length - 1, 0.0, pltpu.roll(p, length - 1, 0))
        return prev * w[0:1] + p * w[1:2] + nxt * w[2:3]

    z = short_conv(pv_ref, wv_ref)
    gates = (short_conv(p1_ref, w1_ref), short_conv(p2_ref, w2_ref))
    fwd = fwd_ref[...]
    inv = inv_ref[...]
    for o in range(HY_ORDER):
        zf = jnp.dot(fwd, z.astype(BF16), preferred_element_type=F32)
        zr, zi = zf[:length], zf[length:]
        kr, ki = k_ref[o, :length, :], k_ref[o, length:, :]
        ii = zi * ki
        pr = zr * kr - jnp.where(row == 0, 0.0, ii)
        pi = jnp.where(row == 0, ii, zr * ki + zi * kr)
        conv = jnp.dot(inv, jnp.concatenate([pr, pi], axis=0).astype(BF16), preferred_element_type=F32)
        z = gates[o] * (conv + z * skip_ref[o:o + 1, :])
    o_ref[...] = z.astype(o_ref.dtype)


def hyena_conv(p, row0, n_seq, length, spectrum, w_short, wl, skip, td):
    fwd, inv = _dft_matrices(length)
    nj = D_MODEL // td
    b0 = row0 // length
    p_spec = lambda c: pl.BlockSpec((length, td), lambda b, j: (b0 + b, c * nj + j))
    w_spec = lambda c: pl.BlockSpec((None, HY_SHORT_W, td), lambda b, j: (wl, 0, c * nj + j))
    return pl.pallas_call(
        functools.partial(_hy_conv_kernel, length=length),
        grid=(n_seq, nj),
        in_specs=[
            p_spec(0), p_spec(1), p_spec(2), w_spec(0), w_spec(1), w_spec(2),
            pl.BlockSpec((HY_ORDER, 2 * length, td), lambda b, j: (0, 0, j)),
            pl.BlockSpec((None, HY_ORDER, td), lambda b, j: (wl, 0, j)),
            pl.BlockSpec((2 * length, length), lambda b, j: (0, 0)),
            pl.BlockSpec((length, 2 * length), lambda b, j: (0, 0)),
        ],
        out_specs=pl.BlockSpec((length, td), lambda b, j: (b, j)),
        out_shape=jax.ShapeDtypeStruct((n_seq * length, D_MODEL), BF16),
        compiler_params=_cparams(("arbitrary", "arbitrary")),
        name="hyena_conv",
    )(p, p, p, w_short, w_short, w_short, spectrum, skip, fwd, inv)


def hyena_mixer(h, wl, w_in, w_short, w1, b1, w_hid, b_hid, freq, w_filt_out, skip):
    p = matmul(h, w_in, wl)
    filt = (w1, b1, w_hid, b_hid, freq, w_filt_out)
    z_p = hyena_conv(p, 0, BATCH, SEQ, hyena_filter_spectrum(SEQ, wl, *filt), w_short, wl, skip, td=D_MODEL)
    z_s = hyena_conv(p, T_PROMPT, DEC_BATCH, DEC_SEQ, hyena_filter_spectrum(DEC_SEQ, wl, *filt), w_short, wl, skip, td=512)
    return jnp.concatenate([z_p, z_s], axis=0)


SSD_Q = 128
SSD_R = SSD_HEADS // SSD_GROUPS
SSD_GW = SSD_R * SSD_HEADDIM


def _shift_rows(x, s, row, length):
    if s == 0:
        return x
    rolled = pltpu.roll(x, (-s) % length, 0)
    ok = (row + s >= 0) & (row + s < length)
    return jnp.where(ok, rolled, 0.0)


def _silu(x):
    return x * jax.nn.sigmoid(x)


def _softplus(x):
    return jnp.maximum(x, 0.0) + jnp.log1p(jnp.exp(-jnp.abs(x)))


def _ssd_kernel(*refs, length, has_init, emit_state):
    (z_ref, x_ref, b_ref, c_ref, wx_ref, wb_ref, wc_ref, bx_ref, bb_ref, bc_ref,
     dtc_ref, dtr_ref, dbc_ref, dbr_ref, alc_ref, alr_ref, dsk_ref, nw_ref, exp_ref) = refs[:19]
    pos = 19
    if has_init:
        s0_ref = refs[pos]
        pos += 1
    o_ref = refs[pos]
    pos += 1
    if emit_state:
        so_ref = refs[pos]
        pos += 1
    xc_s, bc_s, cc_s, y_s, st_s = refs[pos:pos + 5]

    hp = functools.partial(jnp.dot, precision=HIGHEST, preferred_element_type=F32)

    def conv_silu(p_ref, w_ref, bias_ref):
        p = p_ref[...]
        w = w_ref[...]
        row = lax.broadcasted_iota(jnp.int32, p.shape, 0)
        acc = bias_ref[...] + _shift_rows(p, -2, row, length) * w[0:1]
        for k in range(1, SSD_CONV_W):
            acc = acc + _shift_rows(p, k - 2, row, length) * w[k:k + 1]
        return _silu(acc)

    xc_s[...] = conv_silu(x_ref, wx_ref, bx_ref)
    bc_s[...] = conv_silu(b_ref, wb_ref, bb_ref)
    cc_s[...] = conv_silu(c_ref, wc_ref, bc_ref)

    expand = exp_ref[...]
    ti = lax.broadcasted_iota(jnp.int32, (SSD_Q, SSD_Q), 0)
    si = lax.broadcasted_iota(jnp.int32, (SSD_Q, SSD_Q), 1)
    lane = lax.broadcasted_iota(jnp.int32, (SSD_Q, 2 * SSD_HEADDIM), 1)
    n_chunks = length // SSD_Q

    for d in range(2):
        if has_init:
            st_s[d] = s0_ref[d]
        else:
            st_s[d] = jnp.zeros((SSD_STATE, SSD_GW), F32)

    def chunk(ci, carry):
        for d in range(2):
            causal = (si <= ti) if d == 0 else (si >= ti)
            tri = causal.astype(F32)
            tri_t = ((ti <= si) if d == 0 else (ti >= si)).astype(F32)
            a_col = -jnp.exp(alc_ref[d])
            a_row = -jnp.exp(alr_ref[d])
            c = ci if d == 0 else n_chunks - 1 - ci
            r0 = pl.multiple_of(c * SSD_Q, SSD_Q)
            rows = pl.ds(r0, SSD_Q)
            dt_c = _softplus(dtc_ref[d, rows, :] + dbc_ref[d])
            dt_r = _softplus(dtr_ref[d, :, rows] + dbr_ref[d])
            acs = hp(tri, dt_c * a_col)
            acs_r = hp(dt_r * a_row, tri_t)
            end = acs[SSD_Q - 1:SSD_Q, :] if d == 0 else acs[0:1, :]
            e_acs = hp(jnp.exp(acs), expand)
            w_in = hp(jnp.exp(end - acs) * dt_c, expand)
            e_end = e_acs[SSD_Q - 1:SSD_Q, :] if d == 0 else e_acs[0:1, :]
            xq = xc_s[rows, :]
            bq = bc_s[rows, :]
            cq = cc_s[rows, :].astype(BF16)
            cb = _nt_dot(cq, bq.astype(BF16))
            st = st_s[d]
            y = e_acs * jnp.dot(cq, st.astype(BF16), preferred_element_type=F32)
            xb = xq.astype(BF16)
            pieces = []
            for pr in range(SSD_R // 2):
                xp = xb[:, pr * 2 * SSD_HEADDIM:(pr + 1) * 2 * SSD_HEADDIM]
                acc = None
                for sub in range(2):
                    r = 2 * pr + sub
                    seg = acs[:, r:r + 1] - acs_r[r:r + 1, :]
                    w = jnp.where(causal, jnp.exp(jnp.where(causal, seg, 0.0)) * cb * dt_r[r:r + 1, :], 0.0)
                    in_head = (lane < SSD_HEADDIM) if sub == 0 else (lane >= SSD_HEADDIM)
                    part = jnp.dot(w.astype(BF16), jnp.where(in_head, xp, jnp.zeros_like(xp)),
                                   preferred_element_type=F32)
                    acc = part if acc is None else acc + part
                pieces.append(acc)
            y_s[d, rows, :] = y + jnp.concatenate(pieces, axis=1)
            upd = lax.dot_general(bq.astype(BF16), (xq * w_in).astype(BF16), (((0,), (0,)), ((), ())),
                                  preferred_element_type=F32)
            st_s[d] = e_end * st + upd
        return carry

    lax.fori_loop(0, n_chunks, chunk, 0)
    if emit_state:
        for d in range(2):
            so_ref[d] = st_s[d]

    y = (xc_s[...] * dsk_ref[...] + y_s[0] + y_s[1]) * _silu(z_ref[...])
    y = y * lax.rsqrt(jnp.mean(y * y, axis=-1, keepdims=True) + EPS) * nw_ref[...]
    o_ref[...] = y.astype(o_ref.dtype)


def ssd_scan(proj, row0, n_seq, length, dt_col, dt_row, wl, conv_w, conv_b, dtb_col, dtb_row, al_col, al_row,
             d_skip_ch, norm_w, s0, emit_state):
    b0 = row0 // length
    gw_blk = SSD_INNER // SSD_GW
    n_blk = SSD_GN // SSD_STATE
    seq_cols = lambda width, base: pl.BlockSpec((length, width), lambda b, g: (b0 + b, base + g))
    par = lambda rows_, width, base: pl.BlockSpec((None, rows_, width), lambda b, g: (wl, 0, base + g))
    in_specs = [
        seq_cols(SSD_GW, 0),
        seq_cols(SSD_GW, gw_blk),
        seq_cols(SSD_STATE, 2 * SSD_INNER // SSD_STATE),
        seq_cols(SSD_STATE, 2 * SSD_INNER // SSD_STATE + n_blk),
        par(SSD_CONV_W, SSD_GW, 0), par(SSD_CONV_W, SSD_STATE, SSD_INNER // SSD_STATE),
        par(SSD_CONV_W, SSD_STATE, SSD_INNER // SSD_STATE + n_blk),
        par(1, SSD_GW, 0), par(1, SSD_STATE, SSD_INNER // SSD_STATE), par(1, SSD_STATE, SSD_INNER // SSD_STATE + n_blk),
        pl.BlockSpec((2, None, length, SSD_R), lambda b, g: (0, g, b0 + b, 0)),
        pl.BlockSpec((2, None, SSD_R, length), lambda b, g: (0, g, 0, b0 + b)),
        pl.BlockSpec((2, None, 1, SSD_R), lambda b, g: (0, g, 0, 0)),
        pl.BlockSpec((2, None, SSD_R, 1), lambda b, g: (0, g, 0, 0)),
        pl.BlockSpec((2, None, 1, SSD_R), lambda b, g: (0, g, 0, 0)),
        pl.BlockSpec((2, None, SSD_R, 1), lambda b, g: (0, g, 0, 0)),
        par(1, SSD_GW, 0), par(1, SSD_GW, 0),
        pl.BlockSpec((SSD_R, SSD_GW), lambda b, g: (0, 0)),
    ]
    expand = jnp.asarray(np.kron(np.eye(SSD_R), np.ones((1, SSD_HEADDIM))), F32)
    args = [proj, proj, proj, proj, conv_w, conv_w, conv_w, conv_b, conv_b, conv_b,
            dt_col, dt_row, dtb_col, dtb_row, al_col, al_row, d_skip_ch, norm_w, expand]
    if s0 is not None:
        in_specs.append(pl.BlockSpec((None, None, 2, SSD_STATE, SSD_GW), lambda b, g: (b, g, 0, 0, 0)))
        args.append(s0)
    out_specs = [pl.BlockSpec((length, SSD_GW), lambda b, g: (b, g))]
    out_shape = [jax.ShapeDtypeStruct((n_seq * length, SSD_INNER), BF16)]
    if emit_state:
        out_specs.append(pl.BlockSpec((None, None, 2, SSD_STATE, SSD_GW), lambda b, g: (b, g, 0, 0, 0)))
        out_shape.append(jax.ShapeDtypeStruct((n_seq, SSD_GROUPS, 2, SSD_STATE, SSD_GW), F32))
    return pl.pallas_call(
        functools.partial(_ssd_kernel, length=length, has_init=s0 is not None, emit_state=emit_state),
        grid=(n_seq, SSD_GROUPS),
        in_specs=in_specs,
        out_specs=out_specs,
        out_shape=out_shape,
        scratch_shapes=[
            pltpu.VMEM((length, SSD_GW), F32), pltpu.VMEM((length, SSD_STATE), F32),
            pltpu.VMEM((length, SSD_STATE), F32), pltpu.VMEM((2, length, SSD_GW), F32),
            pltpu.VMEM((2, SSD_STATE, SSD_GW), F32),
        ],
        compiler_params=_cparams(("arbitrary", "arbitrary")),
        name="ssd_scan",
    )(*args)


def _ssd_state_to_kernel(s):
    b = s.shape[0]
    s = s.reshape(b, 2, SSD_GROUPS, SSD_R, SSD_HEADDIM, SSD_STATE)
    return s.transpose(0, 2, 1, 5, 3, 4).reshape(b, SSD_GROUPS, 2, SSD_STATE, SSD_GW)


def _ssd_state_from_kernel(s):
    b = s.shape[0]
    s = s.reshape(b, SSD_GROUPS, 2, SSD_STATE, SSD_R, SSD_HEADDIM)
    return s.transpose(0, 2, 1, 4, 5, 3).reshape(b, 2, SSD_HEADS, SSD_HEADDIM, SSD_STATE)


def ssd_mixer(h, wl, w_in, conv_w, conv_b, dt_bias, a_log, d_skip, norm_w, state_in):
    proj = matmul(h, w_in, wl, tn=1152)
    t = proj.shape[0]
    dt_raw = proj[:, SSD_INNER + SSD_CONV_DIM:].reshape(t, 2, SSD_GROUPS, SSD_R)
    dt_col = dt_raw.transpose(1, 2, 0, 3)
    dt_row = dt_raw.transpose(1, 2, 3, 0)
    per_group = lambda p: p[wl].reshape(2, SSD_GROUPS, SSD_R)
    dtb, al = per_group(dt_bias), per_group(a_log)
    small = (dtb[:, :, None, :], dtb[:, :, :, None], al[:, :, None, :], al[:, :, :, None])
    n_ssd = conv_b.shape[0]
    common = (wl, conv_w, conv_b.reshape(n_ssd, 1, SSD_CONV_DIM)) + small + (
        jnp.repeat(d_skip, SSD_HEADDIM, axis=-1).reshape(n_ssd, 1, SSD_INNER), norm_w.reshape(n_ssd, 1, SSD_INNER))
    y_p, st = ssd_scan(proj, 0, BATCH, SEQ, dt_col, dt_row, *common, None, True)
    (y_s,) = ssd_scan(proj, T_PROMPT, DEC_BATCH, DEC_SEQ, dt_col, dt_row, *common,
                      _ssd_state_to_kernel(state_in), False)
    return jnp.concatenate([y_p, y_s], axis=0), _ssd_state_from_kernel(st)


HG_HALF = HG_CHUNK // 2
HG_UNROLL = 4


def _block_cumsum(x, row, reverse):
    n = x.shape[0]
    within = row % HG_CHUNK
    step = 1
    while step < HG_CHUNK:
        if reverse:
            shifted = pltpu.roll(x, n - step, 0)
            ok = within < HG_CHUNK - step
        else:
            shifted = pltpu.roll(x, step, 0)
            ok = within >= step
        x = x + jnp.where(ok, shifted, 0.0)
        step *= 2
    return x


def _hg_kernel(*refs, length, has_init, emit_state):
    q_ref, ff_ref, fb_ref, v_ref, gate_ref, lb_ref, gn_ref = refs[:7]
    pos = 7
    if has_init:
        s0_ref = refs[pos]
        pos += 1
    o_ref = refs[pos]
    pos += 1
    if emit_state:
        so_ref = refs[pos]
        pos += 1
    qs_s, g_s, k_s, o_s, st_s = refs[pos:pos + 5]

    row = lax.broadcasted_iota(jnp.int32, (length, HG_DK), 0)
    sub = lax.broadcasted_iota(jnp.int32, (HG_HALF, HG_DK), 0)
    qs_s[...] = _silu(q_ref[...])
    n_blocks = length // HG_CHUNK

    for d, f_ref in enumerate((ff_ref, fb_ref)):
        lb = lb_ref[d]
        f = f_ref[...]
        log_sig = jnp.minimum(f, 0.0) - jnp.log1p(jnp.exp(-jnp.abs(f)))
        a = jnp.log(lb)
        b = jnp.log1p(-lb) + log_sig
        log_g = jnp.maximum(a, b) + jnp.log1p(jnp.exp(-jnp.abs(a - b)))
        g_s[d] = _block_cumsum(log_g, row, reverse=(d == 1))
        k_s[d] = (1.0 - lb) * jax.nn.sigmoid(-f)
        if has_init:
            st_s[d] = s0_ref[d].T
        else:
            st_s[d] = jnp.zeros((HG_DV, HG_DK), F32)

    def block(ci, carry):
        for d in range(2):
            c = ci if d == 0 else n_blocks - 1 - ci
            rows = pl.ds(pl.multiple_of(c * HG_CHUNK, HG_CHUNK), HG_CHUNK)
            gc = g_s[d, rows, :]
            q = qs_s[rows, :]
            k = k_s[d, rows, :]
            v = v_ref[rows, :]
            st = st_s[d]
            g_end = gc[HG_CHUNK - 1:HG_CHUNK, :] if d == 0 else gc[0:1, :]
            o = _nt_dot((q * jnp.exp(gc)).astype(BF16), st.astype(BF16))
            halves = []
            for hh in range(2):
                gh = gc[hh * HG_HALF:(hh + 1) * HG_HALF]
                qh = q[hh * HG_HALF:(hh + 1) * HG_HALF]
                acc = o[hh * HG_HALF:(hh + 1) * HG_HALF]
                for s in range(HG_CHUNK):
                    lo, hi = hh * HG_HALF, (hh + 1) * HG_HALF - 1
                    if (d == 0 and s > hi) or (d == 1 and s < lo):
                        continue
                    w = qh * k[s:s + 1] * jnp.exp(jnp.minimum(gh - gc[s:s + 1], 0.0))
                    if d == 0 and s > lo:
                        w = jnp.where(sub >= s - lo, w, 0.0)
                    if d == 1 and s < hi:
                        w = jnp.where(sub <= s - lo, w, 0.0)
                    acc = acc + jnp.sum(w, axis=-1, keepdims=True) * v[s:s + 1]
                halves.append(acc)
            o_s[d, rows, :] = jnp.concatenate(halves, axis=0)
            k_hat = k * jnp.exp(g_end - gc)
            upd = lax.dot_general(v.astype(BF16), k_hat.astype(BF16), (((0,), (0,)), ((), ())),
                                  preferred_element_type=F32)
            st_s[d] = st * jnp.exp(g_end) + upd
        return carry

    lax.fori_loop(0, n_blocks, block, 0, unroll=HG_UNROLL)
    if emit_state:
        for d in range(2):
            so_ref[d] = st_s[d].T

    o = o_s[0] + o_s[1]
    o = o * lax.rsqrt(jnp.mean(o * o, axis=-1, keepdims=True) + EPS) * gn_ref[...]
    o_ref[...] = (o * _silu(gate_ref[...])).astype(o_ref.dtype)


def hgrn2_scan(proj, row0, n_seq, length, lb, g_norm, wl, s0, emit_state):
    b0 = row0 // length
    col = lambda base: pl.BlockSpec((length, HG_DK), lambda b, h: (b0 + b, base * HG_HEADS + h))
    in_specs = [col(0), col(1), col(2), col(3), col(4),
                pl.BlockSpec((2, 1, HG_DK), lambda b, h: (0, 0, h)),
                pl.BlockSpec((None, 1, HG_DV), lambda b, h: (wl, 0, 0))]
    args = [proj, proj, proj, proj, proj, lb, g_norm]
    state_spec = pl.BlockSpec((None, 2, None, HG_DK, HG_DV), lambda b, h: (b, 0, h, 0, 0))
    if s0 is not None:
        in_specs.append(state_spec)
        args.append(s0)
    out_specs = [pl.BlockSpec((length, HG_DV), lambda b, h: (b, h))]
    out_shape = [jax.ShapeDtypeStruct((n_seq * length, D_MODEL), BF16)]
    if emit_state:
        out_specs.append(state_spec)
        out_shape.append(jax.ShapeDtypeStruct((n_seq, 2, HG_HEADS, HG_DK, HG_DV), F32))
    return pl.pallas_call(
        functools.partial(_hg_kernel, length=length, has_init=s0 is not None, emit_state=emit_state),
        grid=(n_seq, HG_HEADS),
        in_specs=in_specs,
        out_specs=out_specs,
        out_shape=out_shape,
        scratch_shapes=[pltpu.VMEM((length, HG_DK), F32), pltpu.VMEM((2, length, HG_DK), F32),
                        pltpu.VMEM((2, length, HG_DK), F32), pltpu.VMEM((2, length, HG_DV), F32),
                        pltpu.VMEM((2, HG_DV, HG_DK), F32)],
        compiler_params=_cparams(("arbitrary", "arbitrary")),
        name="hgrn2_scan",
    )(*args)


def hgrn2_mixer(h, wl, w_in, lb, g_norm, state_in):
    proj = matmul(h, w_in, wl)
    lb3 = lb.reshape(2, 1, HG_FDIM)
    gn3 = g_norm.reshape(-1, 1, HG_DV)
    o_p, st = hgrn2_scan(proj, 0, BATCH, SEQ, lb3, gn3, wl, None, True)
    (o_s,) = hgrn2_scan(proj, T_PROMPT, DEC_BATCH, DEC_SEQ, lb3, gn3, wl, state_in, False)
    return jnp.concatenate([o_p, o_s], axis=0), st


def _final_norm_kernel(x_ref, g_ref, o_ref):
    x = x_ref[...]
    o_ref[...] = x * lax.rsqrt(jnp.mean(x * x, axis=-1, keepdims=True) + EPS) * g_ref[...]


def final_norm(x, gain):
    t = x.shape[0]
    return pl.pallas_call(
        _final_norm_kernel,
        grid=(t // NORM_TM,),
        in_specs=[pl.BlockSpec((NORM_TM, D_MODEL), lambda i: (i, 0)), pl.BlockSpec((1, D_MODEL), lambda i: (0, 0))],
        out_specs=pl.BlockSpec((NORM_TM, D_MODEL), lambda i: (i, 0)),
        out_shape=jax.ShapeDtypeStruct((t, D_MODEL), F32),
        compiler_params=_cparams(("arbitrary",)),
        name="final_norm",
    )(x, gain.reshape(1, D_MODEL))


def kernel(x_prompt, x_sample, cache_attn_k, cache_attn_v, state_hgrn, state_ssd, c, c_ctx, ada_w, ada_b, norm_mix, norm_ffn, norm_final, hy_w_in, hy_w_short, hy_filt_w1, hy_filt_b1, hy_filt_w_hid, hy_filt_b_hid, hy_filt_freq, hy_filt_w_out, hy_skip, hy_w_out, hg_w_in, hg_lb, hg_norm, hg_w_o, ssd_w_in, ssd_conv_w, ssd_conv_b, ssd_dt_bias, ssd_a_log, ssd_d, ssd_norm, ssd_w_out, at_w_qkv, at_sink, at_w_o, moe_w_router, moe_b_router, moe_w_in, moe_b_in, moe_w_out, moe_b_out):
    x = jnp.concatenate([x_prompt.reshape(T_PROMPT, D_MODEL), x_sample.reshape(T_SAMPLE, D_MODEL)], axis=0)
    cvec = jnp.zeros((N_CVEC, D_MODEL), F32).at[0].set(c_ctx).at[1:1 + DEC_BATCH].set(c)
    mods = ada_mods(cvec, ada_w, ada_b)
    lbs = jax.nn.softmax(hg_lb.astype(F32), axis=1)
    lbs = jnp.cumsum(lbs, axis=1) - lbs[:, :1]
    w_router_t = moe_w_router.transpose(0, 2, 1)
    new_k, new_v, new_hg, new_ssd = [], [], [], []
    for li in range(DEPTH):
        kind, j = li % N_MIXERS, li // N_MIXERS
        h = norm_modulate(x, norm_mix, mods, li, 0)
        if kind == 0:
            y = hyena_mixer(h, j, hy_w_in, hy_w_short, hy_filt_w1, hy_filt_b1, hy_filt_w_hid, hy_filt_b_hid,
                            hy_filt_freq, hy_filt_w_out, hy_skip)
            w_o = hy_w_out
        elif kind == 1:
            y, st = hgrn2_mixer(h, j, hg_w_in, lbs[:, li], hg_norm, state_hgrn[:, j])
            new_hg.append(st)
            w_o = hg_w_o
        elif kind == 2:
            y, st = ssd_mixer(h, j, ssd_w_in, ssd_conv_w, ssd_conv_b, ssd_dt_bias, ssd_a_log, ssd_d, ssd_norm,
                              state_ssd[:, j])
            new_ssd.append(st)
            w_o = ssd_w_out
        else:
            y, kc, vc = attention_mixer(h, at_w_qkv, j, at_sink[j], cache_attn_k[:, j], cache_attn_v[:, j])
            new_k.append(kc)
            new_v.append(vc)
            w_o = at_w_o
        x = matmul_residual(y, w_o, j, x, mods, li, 2)
        x = moe_layer(x, li, mods, norm_ffn, w_router_t, moe_b_router, moe_w_in, moe_b_in, moe_w_out, moe_b_out)
    y = final_norm(x, norm_final)
    y_prompt = y[:T_PROMPT].reshape(BATCH, SEQ, D_MODEL)
    y_sample = y[T_PROMPT:].reshape(DEC_BATCH, DEC_SEQ, D_MODEL)
    return (y_prompt, y_sample, jnp.stack(new_k, axis=1), jnp.stack(new_v, axis=1),
            jnp.stack(new_hg, axis=1), jnp.stack(new_ssd, axis=1))
```
